```python
import math
import jax
import jax.numpy as jnp
from jax import lax
import numpy as np

D_MODEL = 2048
BATCH = 16
SEQ = 2048
DEPTH = 4

N_EVEN = (DEPTH + 1) // 2
N_ODD = DEPTH // 2

SB_HEADS = 8
SB_HEAD_DIM = 128
DIFF_HEADS = 8
DIFF_QK_DIM = 64
DIFF_V_DIM = 2 * DIFF_QK_DIM
SB_WIDTH = SB_HEADS * SB_HEAD_DIM
DIFF_QK_WIDTH = DIFF_HEADS * 2 * DIFF_QK_DIM
DIFF_V_WIDTH = DIFF_HEADS * DIFF_V_DIM
MIX_WIDTH = SB_WIDTH + DIFF_V_WIDTH
PROJ_WIDTH = 3 * SB_WIDTH + 2 * DIFF_QK_WIDTH + DIFF_V_WIDTH
Q_BLOCK = 128
RWKV_HEAD_DIM = 64
RWKV_HEADS = D_MODEL // RWKV_HEAD_DIM
DECAY_LORA = 96
AAA_LORA = 96
MV_LORA = 64
GATE_LORA = 256
RWKV_GN_EPS = 64e-5
N_EXPERTS = 16
N_GROUPS = 4
EXPERTS_PER_GROUP = N_EXPERTS // N_GROUPS
TOP_K = 2
D_EXPERT = D_MODEL // 2
MOE_BLOCK = 128
NORM_EPS = 1e-6

kernel_name = 'hybrid_sb_diff_rwkv7_grouped_moe'


def rms_norm(x, g, eps=NORM_EPS):
    xf = x.astype(jnp.float32)
    return xf * lax.rsqrt(jnp.mean(xf * xf, axis=-1, keepdims=True) + eps) * g.astype(jnp.float32)


def alibi_slopes(n):
    return 2.0 ** (-8.0 * (jnp.arange(n, dtype=jnp.float32) + 1.0) / n)


def hybrid_attention(h, w_in, w_out, diff_lambda, subln_g, lambda_init):
    b, s, _ = h.shape
    f32 = jnp.float32
    proj = jnp.einsum('bsd,de->bse', h, w_in)

    def heads_a(z):
        return z.reshape(b, s, SB_HEADS, SB_HEAD_DIM).transpose(0, 2, 1, 3).astype(f32)

    def heads_b(z):
        return z.reshape(b, s, DIFF_HEADS, 2, DIFF_QK_DIM).transpose(0, 2, 3, 1, 4).astype(f32)

    q_a = heads_a(proj[..., :SB_WIDTH])
    k_a = heads_a(proj[..., SB_WIDTH:2 * SB_WIDTH])
    v_a = heads_a(proj[..., 2 * SB_WIDTH:3 * SB_WIDTH])
    off = 3 * SB_WIDTH
    q_b = heads_b(proj[..., off:off + DIFF_QK_WIDTH])
    k_b = heads_b(proj[..., off + DIFF_QK_WIDTH:off + 2 * DIFF_QK_WIDTH])
    v_b = proj[..., off + 2 * DIFF_QK_WIDTH:].reshape(b, s, DIFF_HEADS, DIFF_V_DIM).transpose(0, 2, 1, 3).astype(f32)

    lam = diff_lambda.astype(f32)
    lam = jnp.exp(jnp.sum(lam[0] * lam[1])) - jnp.exp(jnp.sum(lam[2] * lam[3])) + lambda_init
    slopes = alibi_slopes(DIFF_HEADS)[None, :, None, None, None]
    key_pos = jnp.arange(s, dtype=jnp.int32)
    scale_a = SB_HEAD_DIM ** -0.5
    scale_b = DIFF_QK_DIM ** -0.5

    def query_block(i):
        t0 = i * Q_BLOCK
        qa = lax.dynamic_slice_in_dim(q_a, t0, Q_BLOCK, axis=2)
        qb = lax.dynamic_slice_in_dim(q_b, t0, Q_BLOCK, axis=3)
        rel = (t0 + jnp.arange(Q_BLOCK, dtype=jnp.int32))[:, None] - key_pos[None, :]
        strict = rel > 0
        z = jnp.einsum('bhqd,bhkd->bhqk', qa, k_a) * scale_a
        log_keep = jnp.where(strict, jax.nn.log_sigmoid(-z), 0.0)
        log_between = lax.cumsum(log_keep, axis=3, reverse=True) - log_keep
        w_a = jnp.where(strict, jnp.exp(jax.nn.log_sigmoid(z) + log_between), 0.0)
        o_a = jnp.einsum('bhqk,bhkd->bhqd', w_a, v_a)
        zb = jnp.einsum('bhcqd,bhckd->bhcqk', qb, k_b) * scale_b - slopes * rel.astype(f32)
        zb = jnp.where(rel >= 0, zb, -jnp.inf)
        p = jax.nn.softmax(zb, axis=-1)
        w_b = p[:, :, 0] - lam * p[:, :, 1]
        o_b = jnp.einsum('bhqk,bhkd->bhqd', w_b, v_b)
        return o_a, o_b

    o_a, o_b = lax.map(query_block, jnp.arange(s // Q_BLOCK, dtype=jnp.int32))
    o_a = o_a.transpose(1, 0, 3, 2, 4).reshape(b, s, SB_WIDTH)
    o_b = o_b.transpose(1, 0, 3, 2, 4).reshape(b, s, DIFF_HEADS, DIFF_V_DIM)
    o_b = rms_norm(o_b, subln_g) * (1.0 - lambda_init)
    o = jnp.concatenate([o_a, o_b.reshape(b, s, DIFF_V_WIDTH)], axis=-1)
    return jnp.einsum('bse,ed->bsd', o, w_out)


def wkv7(r, decay, k, v, a, bvec):
    bsz, _, nh, n = r.shape

    def step(state, inp):
        r_t, w_t, k_t, v_t, a_t, b_t = inp
        sa = jnp.einsum('bhij,bhj->bhi', state, a_t)
        state = (state * w_t[:, :, None, :] + sa[..., None] * b_t[:, :, None, :]
                 + v_t[..., None] * k_t[:, :, None, :])
        return state, jnp.einsum('bhij,bhj->bhi', state, r_t)

    xs = tuple(jnp.moveaxis(z.astype(jnp.float32), 1, 0) for z in (r, decay, k, v, a, bvec))
    _, y = lax.scan(step, jnp.zeros((bsz, nh, n, n), jnp.float32), xs)
    return jnp.moveaxis(y, 0, 1)


def rwkv7_time_mix(h, v_first, mix, w_rkv, w_out, w0, w_la, w_lb, a0, a_la, a_lb, g_la, g_lb,
                   k_k, k_a, r_k, lnx_g, lnx_b, v_res):
    b, s, d = h.shape
    f32 = jnp.float32
    dx = jnp.pad(h[:, :-1], ((0, 0), (1, 0), (0, 0))) - h

    def lerp(i):
        return h + dx * mix[i]

    r = lerp(0) @ w_rkv[0]
    xw = lerp(1)
    k = lerp(2) @ w_rkv[1]
    xv = lerp(3)
    v = xv @ w_rkv[2]
    a = jax.nn.sigmoid(a0 + (lerp(4) @ a_la) @ a_lb)
    g = jax.nn.sigmoid(lerp(5) @ g_la) @ g_lb
    log_log_decay = -jax.nn.softplus(-(w0 + jnp.tanh(xw @ w_la) @ w_lb)) - 0.5
    decay = jnp.exp(-jnp.exp(log_log_decay.astype(f32)))
    if v_res is None:
        v_first = v
    else:
        v0, v_la, v_lb = v_res
        v = v + (v_first - v) * jax.nn.sigmoid(v0 + (xv @ v_la) @ v_lb)

    def heads(z):
        return z.reshape(b, s, RWKV_HEADS, RWKV_HEAD_DIM).astype(f32)

    kk = heads(k * k_k)
    kk = kk / jnp.maximum(jnp.sqrt(jnp.sum(kk * kk, axis=-1, keepdims=True)), 1e-12)
    k = k * (1.0 + (a - 1.0) * k_a)
    rh, kh, vh, ah = heads(r), heads(k), heads(v), heads(a)
    y = wkv7(rh, heads(decay), kh, vh, -kk, kk * ah)
    mu = jnp.mean(y, axis=-1, keepdims=True)
    var = jnp.mean(jnp.square(y - mu), axis=-1, keepdims=True)
    y = ((y - mu) * lax.rsqrt(var + RWKV_GN_EPS) * lnx_g.reshape(RWKV_HEADS, RWKV_HEAD_DIM)
         + lnx_b.reshape(RWKV_HEADS, RWKV_HEAD_DIM))
    y = y + jnp.sum(rh * kh * r_k, axis=-1, keepdims=True) * vh
    return (y.reshape(b, s, d) * g) @ w_out, v_first


def grouped_moe(h, router_w, router_bias, w_gate, w_up, w_down):
    n, d = h.shape
    scores = jax.nn.sigmoid(jnp.einsum('nd,de->ne', h.astype(jnp.float32), router_w.astype(jnp.float32)))
    sel = (scores + router_bias.astype(jnp.float32)).reshape(n, N_GROUPS, EXPERTS_PER_GROUP)
    group_score = lax.top_k(sel, 2)[0].sum(-1)
    group = jnp.argmax(group_score, axis=-1).astype(jnp.int32)
    in_group = sel[jnp.arange(n), group]
    _, local = lax.top_k(in_group, TOP_K)
    expert = group[:, None] * EXPERTS_PER_GROUP + local.astype(jnp.int32)
    gate = jnp.take_along_axis(scores, expert, axis=1)
    gate = gate / jnp.sum(gate, axis=-1, keepdims=True)
    nk = n * TOP_K
    flat_e = expert.reshape(nk)
    order = jnp.argsort(flat_e)
    sorted_e = flat_e[order]
    counts = jax.ops.segment_sum(jnp.ones((nk,), jnp.int32), flat_e, num_segments=N_EXPERTS)
    padded = (counts + MOE_BLOCK - 1) // MOE_BLOCK * MOE_BLOCK
    starts = jnp.cumsum(counts) - counts
    pad_ends = jnp.cumsum(padded)
    pad_starts = pad_ends - padded
    dest = pad_starts[sorted_e] + jnp.arange(nk, dtype=jnp.int32) - starts[sorted_e]
    n_slots = -(-nk // MOE_BLOCK) * MOE_BLOCK + N_EXPERTS * MOE_BLOCK
    n_blocks = n_slots // MOE_BLOCK
    slot_tok = jnp.full((n_slots,), n, jnp.int32).at[dest].set((order // TOP_K).astype(jnp.int32))
    slot_gate = jnp.zeros((n_slots,), jnp.float32).at[dest].set(gate.reshape(nk)[order])
    block_start = jnp.arange(n_blocks, dtype=jnp.int32) * MOE_BLOCK
    block_expert = jnp.minimum(jnp.searchsorted(pad_ends, block_start, side='right'), N_EXPERTS - 1)
    h_pad = jnp.concatenate([h, jnp.zeros((1, d), h.dtype)], axis=0)
    xb = h_pad[slot_tok].reshape(n_blocks, MOE_BLOCK, d)

    def expert_block(args):
        xs, e = args
        return (jax.nn.silu(xs @ w_gate[e]) * (xs @ w_up[e])) @ w_down[e]

    yb = lax.map(expert_block, (xb, block_expert)).reshape(n_slots, d)
    out = jax.ops.segment_sum(yb.astype(jnp.float32) * slot_gate[:, None], slot_tok, num_segments=n + 1)
    return out[:n]


def setup_inputs(seed: int = 0) -> dict:
    key = jax.random.key(seed)
    keys = jax.random.split(key, 48)
    counter = [0]

    def next_key():
        k = keys[counter[0]]
        counter[0] += 1
        return k

    def nrm(shape, std):
        return jax.random.normal(next_key(), shape, jnp.float32) * std

    def gain(shape):
        return 1.0 + nrm(shape, 0.02)

    D = D_MODEL
    inv = D ** -0.5
    ramp = jnp.linspace(0.0, 1.0, D, dtype=jnp.float32) ** 0.9
    return {
        'x': nrm((BATCH, SEQ, D), 1.0),
        'c': nrm((BATCH, D), 1.0),
        'w_mod': nrm((DEPTH, D, 6 * D), 0.5 * inv),
        'b_mod': nrm((DEPTH, 6 * D), 0.02),
        'norm1_g': gain((DEPTH, D)),
        'norm2_g': gain((DEPTH, D)),
        'final_g': gain((D,)),
        'attn_w_in': nrm((N_EVEN, D, PROJ_WIDTH), inv),
        'attn_w_out': nrm((N_EVEN, MIX_WIDTH, D), MIX_WIDTH ** -0.5),
        'diff_lambda': nrm((N_EVEN, 4, DIFF_QK_DIM), 0.1),
        'diff_subln_g': gain((N_EVEN, DIFF_V_DIM)),
        'rwkv_mix': jax.random.uniform(next_key(), (N_ODD, 6, D), jnp.float32),
        'rwkv_w_rkv': nrm((N_ODD, 3, D, D), inv),
        'rwkv_w_out': nrm((N_ODD, D, D), inv),
        'rwkv_w0': -6.0 + 5.0 * ramp + nrm((N_ODD, D), 0.1),
        'rwkv_w_lora_a': nrm((N_ODD, D, DECAY_LORA), inv),
        'rwkv_w_lora_b': nrm((N_ODD, DECAY_LORA, D), 0.1 * DECAY_LORA ** -0.5),
        'rwkv_a0': nrm((N_ODD, D), 0.1),
        'rwkv_a_lora_a': nrm((N_ODD, D, AAA_LORA), inv),
        'rwkv_a_lora_b': nrm((N_ODD, AAA_LORA, D), 0.1 * AAA_LORA ** -0.5),
        'rwkv_g_lora_a': nrm((N_ODD, D, GATE_LORA), inv),
        'rwkv_g_lora_b': nrm((N_ODD, GATE_LORA, D), GATE_LORA ** -0.5),
        'rwkv_v0': 1.0 + nrm((N_ODD - 1, D), 0.1),
        'rwkv_v_lora_a': nrm((N_ODD - 1, D, MV_LORA), inv),
        'rwkv_v_lora_b': nrm((N_ODD - 1, MV_LORA, D), 0.1 * MV_LORA ** -0.5),
        'rwkv_k_k': 0.85 + nrm((N_ODD, D), 0.02),
        'rwkv_k_a': 1.0 + nrm((N_ODD, D), 0.02),
        'rwkv_r_k': nrm((N_ODD, RWKV_HEADS, RWKV_HEAD_DIM), 0.1),
        'rwkv_lnx_g': gain((N_ODD, D)),
        'rwkv_lnx_b': nrm((N_ODD, D), 0.02),
        'router_w': nrm((D, N_EXPERTS), inv),
        'router_bias': nrm((N_EXPERTS,), 0.01),
        'moe_w_gate': nrm((DEPTH, N_EXPERTS, D, D_EXPERT), inv),
        'moe_w_up': nrm((DEPTH, N_EXPERTS, D, D_EXPERT), inv),
        'moe_w_down': nrm((DEPTH, N_EXPERTS, D_EXPERT, D), D_EXPERT ** -0.5),
    }


def reference(x, c, w_mod, b_mod, norm1_g, norm2_g, final_g, attn_w_in, attn_w_out, diff_lambda,
              diff_subln_g, rwkv_mix, rwkv_w_rkv, rwkv_w_out, rwkv_w0, rwkv_w_lora_a, rwkv_w_lora_b,
              rwkv_a0, rwkv_a_lora_a, rwkv_a_lora_b, rwkv_g_lora_a, rwkv_g_lora_b, rwkv_v0,
              rwkv_v_lora_a, rwkv_v_lora_b, rwkv_k_k, rwkv_k_a, rwkv_r_k, rwkv_lnx_g, rwkv_lnx_b,
              router_w, router_bias, moe_w_gate, moe_w_up, moe_w_down):
    cond = jax.nn.silu(c.astype(jnp.float32))
    v_first = None
    for l in range(DEPTH):
        mod = cond @ w_mod[l] + b_mod[l]
        shift1, scale1, gate1, shift2, scale2, gate2 = jnp.split(mod[:, None, :], 6, axis=-1)
        h = rms_norm(x, norm1_g[l]) * (1.0 + scale1) + shift1
        if l % 2 == 0:
            e = l // 2
            lambda_init = 0.8 - 0.6 * math.exp(-0.3 * l)
            y = hybrid_attention(h, attn_w_in[e], attn_w_out[e], diff_lambda[e], diff_subln_g[e], lambda_init)
        else:
            o = l // 2
            v_res = None if o == 0 else (rwkv_v0[o - 1], rwkv_v_lora_a[o - 1], rwkv_v_lora_b[o - 1])
            y, v_first = rwkv7_time_mix(h, v_first, rwkv_mix[o], rwkv_w_rkv[o], rwkv_w_out[o], rwkv_w0[o],
                                        rwkv_w_lora_a[o], rwkv_w_lora_b[o], rwkv_a0[o], rwkv_a_lora_a[o],
                                        rwkv_a_lora_b[o], rwkv_g_lora_a[o], rwkv_g_lora_b[o], rwkv_k_k[o],
                                        rwkv_k_a[o], rwkv_r_k[o], rwkv_lnx_g[o], rwkv_lnx_b[o], v_res)
        x = x + gate1 * y
        h = rms_norm(x, norm2_g[l]) * (1.0 + scale2) + shift2
        y = grouped_moe(h.reshape(-1, D_MODEL), router_w, router_bias, moe_w_gate[l], moe_w_up[l], moe_w_down[l])
        x = x + gate2 * y.reshape(x.shape)
    return rms_norm(x, final_g)
```

```python
import functools
import math

import jax
import jax.numpy as jnp
from jax import lax
from jax.experimental import pallas as pl
from jax.experimental.pallas import tpu as pltpu

F32 = jnp.float32
BF16 = jnp.bfloat16
HIGHEST = lax.Precision.HIGHEST

D_MODEL = 2048
SB_HEADS = 8
SB_HEAD_DIM = 128
DIFF_HEADS = 8
DIFF_QK_DIM = 64
DIFF_V_DIM = 128
SB_WIDTH = SB_HEADS * SB_HEAD_DIM
DIFF_QK_WIDTH = DIFF_HEADS * 2 * DIFF_QK_DIM
DIFF_V_WIDTH = DIFF_HEADS * DIFF_V_DIM
PROJ_WIDTH = 3 * SB_WIDTH + 2 * DIFF_QK_WIDTH + DIFF_V_WIDTH
RWKV_HEAD_DIM = 64
RWKV_GN_EPS = 64e-5
N_EXPERTS = 16
N_GROUPS = 4
EXPERTS_PER_GROUP = 4
D_EXPERT = D_MODEL // 2
NORM_EPS = 1e-6

LANES = 128
WKV_CHUNK = 64
WKV_LANES = 256
WKV_HEADS = WKV_LANES // RWKV_HEAD_DIM
MOE_ROWS = 256
NEG_BIG = -1e30
VMEM_LIMIT = 56 * 1024 * 1024


def _cparams(*sem):
    return pltpu.CompilerParams(dimension_semantics=sem, vmem_limit_bytes=VMEM_LIMIT)


def _dot(a, b):
    return jnp.dot(a, b, preferred_element_type=F32)


def _dot_nt(a, b):
    return lax.dot_general(a, b, (((1,), (1,)), ((), ())), preferred_element_type=F32)


def _dot_tn(a, b):
    return lax.dot_general(a, b, (((0,), (0,)), ((), ())), preferred_element_type=F32)


def _softplus(z):
    return jnp.maximum(z, 0.0) + jnp.log1p(jnp.exp(-jnp.abs(z)))


def _modnorm(x, g, scale, shift):
    ms = jnp.mean(x * x, axis=-1, keepdims=True)
    return x * lax.rsqrt(ms + NORM_EPS) * g * (1.0 + scale) + shift


def _mod_spec(which, rows_per_batch_tile):
    return pl.BlockSpec((None, None, 1, D_MODEL), lambda i, *_: (i // rows_per_batch_tile, which, 0, 0))


def _mod_kernel(c_ref, w_ref, b_ref, o_ref):
    c = c_ref[...]
    cond = c * jax.nn.sigmoid(c)
    o_ref[...] = jnp.dot(cond, w_ref[...], preferred_element_type=F32, precision=HIGHEST) + b_ref[...]


def _modulation(c, w_mod, b_mod):
    depth, d, n6 = w_mod.shape
    bsz = c.shape[0]
    tn = 1536
    out = pl.pallas_call(
        _mod_kernel,
        grid=(depth, n6 // tn),
        in_specs=[pl.BlockSpec((bsz, d), lambda l, j: (0, 0)),
                  pl.BlockSpec((None, d, tn), lambda l, j: (l, 0, j)),
                  pl.BlockSpec((None, 1, tn), lambda l, j: (l, 0, j))],
        out_specs=pl.BlockSpec((None, bsz, tn), lambda l, j: (l, 0, j)),
        out_shape=jax.ShapeDtypeStruct((depth, bsz, n6), F32),
        compiler_params=_cparams("arbitrary", "arbitrary"),
        name="adaln_modulation",
    )(c, w_mod, b_mod.reshape(depth, 1, n6))
    return out.reshape(depth, bsz, 6, 1, d)


def _normmm_kernel(x_ref, g_ref, sh_ref, sc_ref, w_ref, o_ref, h_ref):
    @pl.when(pl.program_id(1) == 0)
    def _():
        h_ref[...] = _modnorm(x_ref[...], g_ref[...], sc_ref[...], sh_ref[...]).astype(BF16)

    o_ref[...] = _dot(h_ref[...], w_ref[...]).astype(o_ref.dtype)


def _norm_matmul(x, g, mod_l, w, seq):
    n, d = x.shape
    nout = w.shape[1]
    tm = min(1024, seq)
    tn = 512
    tps = seq // tm
    return pl.pallas_call(
        _normmm_kernel,
        grid=(n // tm, nout // tn),
        in_specs=[pl.BlockSpec((tm, d), lambda i, j: (i, 0)),
                  pl.BlockSpec((1, d), lambda i, j: (0, 0)),
                  _mod_spec(0, tps), _mod_spec(1, tps),
                  pl.BlockSpec((d, tn), lambda i, j: (0, j))],
        out_specs=pl.BlockSpec((tm, tn), lambda i, j: (i, j)),
        out_shape=jax.ShapeDtypeStruct((n, nout), BF16),
        scratch_shapes=[pltpu.VMEM((tm, d), BF16)],
        compiler_params=_cparams("arbitrary", "arbitrary"),
        name="attn_in_proj",
    )(x, g.reshape(1, d), mod_l, mod_l, w)


def _outproj_kernel(*refs, n_lhs):
    lhs = refs[:n_lhs]
    ws = refs[n_lhs:2 * n_lhs]
    x_ref, gate_ref, o_ref = refs[2 * n_lhs:]
    acc = _dot(lhs[0][...], ws[0][...])
    for a_ref, w_ref in zip(lhs[1:], ws[1:]):
        acc = acc + _dot(a_ref[...], w_ref[...])
    o_ref[...] = x_ref[...] + gate_ref[...] * acc


def _out_proj(lhs_list, w, x, mod_l, gate_idx, seq):
    n, d = x.shape
    tm = min(1024, seq)
    tn = 512
    tps = seq // tm
    n_lhs = len(lhs_list)
    kp = w.shape[0] // n_lhs
    in_specs = [pl.BlockSpec((tm, kp), lambda i, j: (i, 0)) for _ in lhs_list]
    in_specs += [pl.BlockSpec((kp, tn), functools.partial(lambda i, j, p: (p, j), p=p)) for p in range(n_lhs)]
    in_specs += [pl.BlockSpec((tm, tn), lambda i, j: (i, j)),
                 pl.BlockSpec((None, None, 1, tn), lambda i, j: (i // tps, gate_idx, 0, j))]
    return pl.pallas_call(
        functools.partial(_outproj_kernel, n_lhs=n_lhs),
        grid=(n // tm, d // tn),
        in_specs=in_specs,
        out_specs=pl.BlockSpec((tm, tn), lambda i, j: (i, j)),
        out_shape=jax.ShapeDtypeStruct((n, d), F32),
        compiler_params=_cparams("arbitrary", "arbitrary"),
        name="out_proj_residual",
    )(*lhs_list, *([w] * n_lhs), x, mod_l)


def _sb_kernel(q_ref, k_ref, v_ref, o_ref, *, tq, tk, scale):
    qi = pl.program_id(2)
    q = q_ref[...]
    jj = lax.broadcasted_iota(jnp.int32, (tk, tk), 0)
    ss = lax.broadcasted_iota(jnp.int32, (tk, tk), 1)
    later = jnp.where(jj > ss, 1.0, 0.0).astype(BF16)
    rel0 = (lax.broadcasted_iota(jnp.int32, (tq, tk), 0) - lax.broadcasted_iota(jnp.int32, (tq, tk), 1))

    def block(kb, carry, acc, masked):
        start = pl.multiple_of(kb * tk, tk)
        k = k_ref[pl.ds(start, tk), :]
        v = v_ref[pl.ds(start, tk), :]
        z = _dot_nt(q, k) * scale
        sp = _softplus(z)
        log_keep = -sp
        if masked:
            strict = (rel0 + (qi * tq - kb * tk)) > 0
            log_keep = jnp.where(strict, log_keep, 0.0)
        hi = log_keep.astype(BF16)
        lo = (log_keep - hi.astype(F32)).astype(BF16)
        cs = _dot(jnp.concatenate([hi, lo], axis=0), later)
        log_between = cs[:tq] + cs[tq:] + carry
        w = jnp.exp(z - sp + log_between)
        if masked:
            w = jnp.where(strict, w, 0.0)
        acc = acc + _dot(w.astype(BF16), v)
        carry = carry + jnp.sum(log_keep, axis=1, keepdims=True)
        return carry, acc

    carry = jnp.zeros((tq, 1), F32)
    acc = jnp.zeros((tq, SB_HEAD_DIM), F32)
    n_full = (qi * tq) // tk
    last = ((qi + 1) * tq - 1) // tk
    for m in range(max(tq // tk, 1)):
        carry, acc = block(last - m, carry, acc, True)

    def body(i, c):
        return block(n_full - 1 - i, c[0], c[1], False)

    carry, acc = lax.fori_loop(0, n_full, body, (carry, acc))
    o_ref[...] = acc.astype(o_ref.dtype)


def _sb_attention(proj, bsz, seq):
    n = proj.shape[0]
    tq = min(256, seq)
    tk = min(128, seq)
    nq = seq // tq
    kern = functools.partial(_sb_kernel, tq=tq, tk=tk, scale=SB_HEAD_DIM ** -0.5)
    return pl.pallas_call(
        kern,
        grid=(bsz, SB_HEADS, nq),
        in_specs=[pl.BlockSpec((tq, LANES), lambda b, h, i: (b * nq + i, h)),
                  pl.BlockSpec((seq, LANES), lambda b, h, i: (b, SB_HEADS + h)),
                  pl.BlockSpec((seq, LANES), lambda b, h, i: (b, 2 * SB_HEADS + h))],
        out_specs=pl.BlockSpec((tq, LANES), lambda b, h, i: (b * nq + i, h)),
        out_shape=jax.ShapeDtypeStruct((n, SB_WIDTH), BF16),
        compiler_params=_cparams("arbitrary", "arbitrary", "arbitrary"),
        name="stick_breaking_attention",
    )(proj, proj, proj)


def _diff_kernel(q_ref, k_ref, v_ref, slope_ref, lam_ref, g_ref, o_ref, *, tq, tk, scale, lambda_init):
    qi = pl.program_id(2)
    q = q_ref[...]
    lane = lax.broadcasted_iota(jnp.int32, (tq, LANES), 1)
    zero = jnp.zeros_like(q)
    q2 = jnp.concatenate([jnp.where(lane < DIFF_QK_DIM, q, zero), jnp.where(lane >= DIFF_QK_DIM, q, zero)], axis=0)
    row = lax.broadcasted_iota(jnp.int32, (2 * tq, tk), 0)
    rel0 = jnp.where(row >= tq, row - tq, row) - lax.broadcasted_iota(jnp.int32, (2 * tq, tk), 1)
    slope = slope_ref[...]

    def block(kb, m, l, acc, masked):
        start = pl.multiple_of(kb * tk, tk)
        k = k_ref[pl.ds(start, tk), :]
        v = v_ref[pl.ds(start, tk), :]
        rel = rel0 + (qi * tq - kb * tk)
        s = _dot_nt(q2, k) * scale - slope * rel.astype(F32)
        if masked:
            s = jnp.where(rel >= 0, s, NEG_BIG)
        m_new = jnp.maximum(m, jnp.max(s, axis=1, keepdims=True))
        alpha = jnp.exp(m - m_new)
        p = jnp.exp(s - m_new)
        l = alpha * l + jnp.sum(p, axis=1, keepdims=True)
        acc = alpha * acc + _dot(p.astype(BF16), v)
        return m_new, l, acc

    m = jnp.full((2 * tq, 1), NEG_BIG, F32)
    l = jnp.zeros((2 * tq, 1), F32)
    acc = jnp.zeros((2 * tq, DIFF_V_DIM), F32)
    n_full = (qi * tq) // tk
    last = ((qi + 1) * tq - 1) // tk
    m, l, acc = lax.fori_loop(0, n_full, lambda kb, c: block(kb, c[0], c[1], c[2], False), (m, l, acc))
    n_masked = max(tq // tk, 1)
    for j in range(n_masked):
        m, l, acc = block(last - (n_masked - 1 - j), m, l, acc, True)

    lam4 = lam_ref[...]
    lam = (jnp.exp(jnp.sum(lam4[0:1] * lam4[1:2], axis=1, keepdims=True))
           - jnp.exp(jnp.sum(lam4[2:3] * lam4[3:4], axis=1, keepdims=True)) + lambda_init)
    o = acc[:tq] / l[:tq] - lam * (acc[tq:] / l[tq:])
    o = o * lax.rsqrt(jnp.mean(o * o, axis=-1, keepdims=True) + NORM_EPS) * g_ref[...] * (1.0 - lambda_init)
    o_ref[...] = o.astype(o_ref.dtype)


def _diff_attention(proj, diff_lambda, subln_g, lambda_init, bsz, seq):
    n = proj.shape[0]
    tq = min(128, seq)
    tk = min(256, seq)
    nq = seq // tq
    qoff = 3 * SB_HEADS
    slopes = 2.0 ** (-8.0 * (jnp.arange(DIFF_HEADS, dtype=F32) + 1.0) / DIFF_HEADS)
    slopes = jnp.broadcast_to(slopes[:, None, None], (DIFF_HEADS, 1, tk))
    kern = functools.partial(_diff_kernel, tq=tq, tk=tk, scale=DIFF_QK_DIM ** -0.5, lambda_init=lambda_init)
    return pl.pallas_call(
        kern,
        grid=(bsz, DIFF_HEADS, nq),
        in_specs=[pl.BlockSpec((tq, LANES), lambda b, h, i: (b * nq + i, qoff + h)),
                  pl.BlockSpec((seq, LANES), lambda b, h, i: (b, qoff + DIFF_HEADS + h)),
                  pl.BlockSpec((seq, LANES), lambda b, h, i: (b, qoff + 2 * DIFF_HEADS + h)),
                  pl.BlockSpec((None, 1, tk), lambda b, h, i: (h, 0, 0)),
                  pl.BlockSpec((4, DIFF_QK_DIM), lambda b, h, i: (0, 0)),
                  pl.BlockSpec((1, DIFF_V_DIM), lambda b, h, i: (0, 0))],
        out_specs=pl.BlockSpec((tq, LANES), lambda b, h, i: (b * nq + i, h)),
        out_shape=jax.ShapeDtypeStruct((n, DIFF_V_WIDTH), BF16),
        compiler_params=_cparams("arbitrary", "arbitrary", "arbitrary"),
        name="differential_attention",
    )(proj, proj, proj, slopes, diff_lambda, subln_g.reshape(1, DIFF_V_DIM))


RKV_TILES = 12
LORA_TILES = 4
RWKV_TN = 512


def _rwkv_proj_kernel(*refs, tm, tiles_per_seq, has_vres):
    (x_ref, xp_ref, g_ref, sh_ref, sc_ref, mix_ref, w_ref, wla_ref, ala_ref, gla_ref,
     wlb_ref, alb_ref, glb_ref, w0_ref, a0_ref) = refs[:15]
    pos = 15
    if has_vres:
        vla_ref, vlb_ref, v0_ref = refs[pos:pos + 3]
        pos += 3
    rkv_ref, logw_ref, a_out_ref, g_out_ref = refs[pos:pos + 4]
    pos += 4
    if has_vres:
        vg_out_ref = refs[pos]
        pos += 1
    lerp_ref, l1w_ref, l1a_ref, l1g_ref = refs[pos:pos + 4]
    pos += 4
    if has_vres:
        l1v_ref = refs[pos]

    i = pl.program_id(0)
    j = pl.program_id(1)

    @pl.when(j == 0)
    def _():
        g, sc, sh = g_ref[...], sc_ref[...], sh_ref[...]
        h = _modnorm(x_ref[...], g, sc, sh)
        hp = _modnorm(xp_ref[...], g, sc, sh)[7:8, :]
        hp = jnp.where(i % tiles_per_seq == 0, jnp.zeros_like(hp), hp)
        first_row = lax.broadcasted_iota(jnp.int32, h.shape, 0) == 0
        dx = jnp.where(first_row, hp, pltpu.roll(h, 1, 0)) - h
        for m in range(6):
            lerp_ref[m] = (h + dx * mix_ref[m:m + 1, :]).astype(BF16)

    @pl.when(j < RKV_TILES)
    def _():
        sel = j // (RKV_TILES // 3)
        idx = jnp.where(sel == 0, 0, sel + 1)
        rkv_ref[...] = _dot(lerp_ref[idx], w_ref[...]).astype(rkv_ref.dtype)

    @pl.when(j == RKV_TILES)
    def _():
        l1w_ref[...] = jnp.tanh(_dot(lerp_ref[1], wla_ref[...])).astype(BF16)
        l1a_ref[...] = _dot(lerp_ref[4], ala_ref[...]).astype(BF16)
        l1g_ref[...] = jax.nn.sigmoid(_dot(lerp_ref[5], gla_ref[...])).astype(BF16)
        if has_vres:
            l1v_ref[...] = _dot(lerp_ref[3], vla_ref[...]).astype(BF16)

    @pl.when(j >= RKV_TILES)
    def _():
        pre = w0_ref[...] + _dot(l1w_ref[...], wlb_ref[...])
        logw_ref[...] = -jnp.exp(-_softplus(-pre) - 0.5)
        a_out_ref[...] = jax.nn.sigmoid(a0_ref[...] + _dot(l1a_ref[...], alb_ref[...])).astype(BF16)
        g_out_ref[...] = _dot(l1g_ref[...], glb_ref[...]).astype(BF16)
        if has_vres:
            vg_out_ref[...] = jax.nn.sigmoid(v0_ref[...] + _dot(l1v_ref[...], vlb_ref[...])).astype(BF16)


def _pad_cols(w, to):
    return jnp.pad(w, ((0, 0), (0, to - w.shape[1])))


def _pad_rows(w, to):
    return jnp.pad(w, ((0, to - w.shape[0]), (0, 0)))


def _rwkv_proj(x, g, mod_l, mix, w_rkv, w0, w_la, w_lb, a0, a_la, a_lb, g_la, g_lb, v_res, seq):
    n, d = x.shape
    tm = min(512, seq)
    tn = RWKV_TN
    tps = seq // tm
    has_vres = v_res is not None
    wcat = jnp.concatenate([w_rkv[0], w_rkv[1], w_rkv[2]], axis=1).astype(BF16)
    lw = LANES
    gl = g_la.shape[1]

    def rkv_col(i, j):
        return (0, jnp.minimum(j, RKV_TILES - 1))

    def lora_col(i, j):
        return (0, jnp.clip(j - RKV_TILES, 0, LORA_TILES - 1))

    def full(shape):
        return pl.BlockSpec(shape, lambda i, j: (0, 0))

    in_specs = [pl.BlockSpec((tm, d), lambda i, j: (i, 0)),
                pl.BlockSpec((8, d), lambda i, j: (jnp.maximum(i * (tm // 8) - 1, 0), 0)),
                full((1, d)), _mod_spec(0, tps), _mod_spec(1, tps), full((6, d)),
                pl.BlockSpec((d, tn), rkv_col),
                full((d, lw)), full((d, lw)), full((d, gl)),
                pl.BlockSpec((lw, tn), lora_col), pl.BlockSpec((lw, tn), lora_col), pl.BlockSpec((gl, tn), lora_col),
                pl.BlockSpec((1, tn), lora_col), pl.BlockSpec((1, tn), lora_col)]
    args = [x, x, g.reshape(1, d), mod_l, mod_l, mix, wcat,
            _pad_cols(w_la, lw).astype(BF16), _pad_cols(a_la, lw).astype(BF16), g_la.astype(BF16),
            _pad_rows(w_lb, lw).astype(BF16), _pad_rows(a_lb, lw).astype(BF16), g_lb.astype(BF16),
            w0.reshape(1, d), a0.reshape(1, d)]
    out_tile = lambda i, j: (i, jnp.clip(j - RKV_TILES, 0, LORA_TILES - 1))
    out_specs = [pl.BlockSpec((tm, tn), lambda i, j: (i, jnp.minimum(j, RKV_TILES - 1))),
                 pl.BlockSpec((tm, tn), out_tile), pl.BlockSpec((tm, tn), out_tile), pl.BlockSpec((tm, tn), out_tile)]
    out_shape = [jax.ShapeDtypeStruct((n, 3 * d), BF16), jax.ShapeDtypeStruct((n, d), F32),
                 jax.ShapeDtypeStruct((n, d), BF16), jax.ShapeDtypeStruct((n, d), BF16)]
    scratch = [pltpu.VMEM((6, tm, d), BF16), pltpu.VMEM((tm, lw), BF16), pltpu.VMEM((tm, lw), BF16),
               pltpu.VMEM((tm, gl), BF16)]
    if has_vres:
        v0, v_la, v_lb = v_res
        in_specs += [full((d, lw)), pl.BlockSpec((lw, tn), lora_col), pl.BlockSpec((1, tn), lora_col)]
        args += [_pad_cols(v_la, lw).astype(BF16), _pad_rows(v_lb, lw).astype(BF16), v0.reshape(1, d)]
        out_specs.append(pl.BlockSpec((tm, tn), out_tile))
        out_shape.append(jax.ShapeDtypeStruct((n, d), BF16))
        scratch.append(pltpu.VMEM((tm, lw), BF16))
    return pl.pallas_call(
        functools.partial(_rwkv_proj_kernel, tm=tm, tiles_per_seq=tps, has_vres=has_vres),
        grid=(n // tm, RKV_TILES + LORA_TILES),
        in_specs=in_specs, out_specs=out_specs, out_shape=out_shape, scratch_shapes=scratch,
        compiler_params=_cparams("arbitrary", "arbitrary"),
        name="rwkv_projections",
    )(*args)


def _split3(x):
    hi = x.astype(BF16)
    r1 = x - hi.astype(F32)
    mid = r1.astype(BF16)
    lo = (r1 - mid.astype(F32)).astype(BF16)
    return hi, mid, lo


def _mm_precise(a, b):
    ah = a.astype(BF16)
    al = (a - ah.astype(F32)).astype(BF16)
    bh = b.astype(BF16)
    bl = (b - bh.astype(F32)).astype(BF16)
    return _dot(ah, bh) + _dot(ah, bl) + _dot(al, bh)


def _unit_lower_inverse(a, row, col):
    def same_block(nb):
        return (row // nb) == (col // nb)

    eye = jnp.where(row == col, 1.0, 0.0)
    b16 = same_block(16)
    ad = jnp.where(b16, a, 0.0)
    a2 = _mm_precise(ad, ad)
    a4 = _mm_precise(a2, a2)
    a8 = _mm_precise(a4, a4)
    t = eye + ad + a2 + _mm_precise(ad, a2)
    t = t + _mm_precise(t, a4)
    t = t + _mm_precise(t, a8)
    prev = b16
    for nb in (32, 64):
        cur = same_block(nb)
        off = jnp.where(jnp.logical_and(cur, jnp.logical_not(prev)), a, 0.0)
        t = t + _mm_precise(_mm_precise(t, off), t)
        prev = cur
    return t


def _wkv_kernel(*refs, n_chunks, has_vres):
    (r_ref, k_ref, v_ref, lw_ref, a_ref, g_ref) = refs[:6]
    pos = 6
    if has_vres:
        vg_ref, vf_ref = refs[pos:pos + 2]
        pos += 2
    kk_ref, ka_ref, rk_ref, lng_ref, lnb_ref = refs[pos:pos + 5]
    pos += 5
    y_ref = refs[pos]
    s_ref = refs[pos + 1]

    L, W = WKV_CHUNK, WKV_LANES
    R = WKV_HEADS * L

    @pl.when(pl.program_id(2) == 0)
    def _():
        s_ref[...] = jnp.zeros_like(s_ref)

    lane = lax.broadcasted_iota(jnp.int32, (L, W), 1)
    head_masks = [(lane // RWKV_HEAD_DIM) == h for h in range(WKV_HEADS)]
    row = lax.broadcasted_iota(jnp.int32, (R, R), 0)
    col = lax.broadcasted_iota(jnp.int32, (R, R), 1)
    strict = row > col
    incl = row >= col
    tt = lax.broadcasted_iota(jnp.int32, (L, L), 0)
    tj = lax.broadcasted_iota(jnp.int32, (L, L), 1)
    upto = jnp.where(tj <= tt, 1.0, 0.0).astype(BF16)

    def stack(z):
        return jnp.concatenate([jnp.where(mk, z, 0.0) for mk in head_masks], axis=0)

    def head_sum(z):
        out = jnp.zeros_like(z)
        for mk in head_masks:
            out = jnp.where(mk, jnp.sum(jnp.where(mk, z, 0.0), axis=1, keepdims=True), out)
        return out

    k_k, k_a, r_k = kk_ref[...], ka_ref[...], rk_ref[...]
    ln_g, ln_b = lng_ref[...], lnb_ref[...]

    def chunk(c, carry):
        rows = pl.ds(pl.multiple_of(c * L, L), L)
        r = r_ref[rows, :].astype(F32)
        k = k_ref[rows, :].astype(F32)
        v = v_ref[rows, :].astype(F32)
        a = a_ref[rows, :].astype(F32)
        lw = lw_ref[rows, :]
        if has_vres:
            v = v + (vf_ref[rows, :].astype(F32) - v) * vg_ref[rows, :].astype(F32)
        kk = k * k_k
        kk = kk / jnp.maximum(jnp.sqrt(head_sum(kk * kk)), 1e-12)
        km = k * (1.0 + (a - 1.0) * k_a)
        bv = kk * a

        hi, mid, lo = _split3(lw)
        cs = _dot(upto, jnp.concatenate([hi, mid, lo], axis=1))
        cl = cs[:, :W] + cs[:, W:2 * W] + cs[:, 2 * W:]
        p_in = jnp.exp(cl)
        p_inv = jnp.exp(-cl)
        p_last = p_in[L - 1:L, :]
        at = -kk * jnp.exp(cl - lw)
        rt = r * p_in
        bt = bv * p_inv
        kt = km * p_inv

        s0 = s_ref[...]
        ar = jnp.concatenate([stack(at), stack(rt)], axis=0).astype(BF16)
        bk = jnp.concatenate([stack(bt), stack(kt)], axis=0).astype(BF16)
        gm = _dot_nt(ar, bk)
        a_ab = jnp.where(strict, gm[:R, :R], 0.0)
        a_ak = jnp.where(strict, gm[:R, R:], 0.0)
        a_rb = jnp.where(incl, gm[R:, :R], 0.0)
        a_rk = jnp.where(incl, gm[R:, R:], 0.0)
        ss = _dot_nt(ar, s0.astype(BF16))
        vx = stack(v)
        vxb = vx.astype(BF16)
        x = ss[:R] + _dot(a_ak.astype(BF16), vxb)
        u = _mm_precise(_unit_lower_inverse(a_ab, row, col), x)
        uv = jnp.concatenate([u.astype(BF16), vxb], axis=0)
        yx = ss[R:] + _dot(jnp.concatenate([a_rb, a_rk], axis=1).astype(BF16), uv)
        y = yx[0:L]
        for h in range(1, WKV_HEADS):
            y = y + yx[h * L:(h + 1) * L]
        bkl = jnp.concatenate([stack(bt * p_last), stack(kt * p_last)], axis=0).astype(BF16)
        s_ref[...] = s0 * p_last + _dot_tn(uv, bkl)

        inv_n = 1.0 / RWKV_HEAD_DIM
        mu = head_sum(y) * inv_n
        dlt = y - mu
        var = head_sum(dlt * dlt) * inv_n
        yn = dlt * lax.rsqrt(var + RWKV_GN_EPS) * ln_g + ln_b
        yn = yn + head_sum(r * km * r_k) * v
        y_ref[rows, :] = (yn * g_ref[rows, :].astype(F32)).astype(y_ref.dtype)
        return carry

    lax.fori_loop(0, n_chunks, chunk, 0)


def _wkv(rkv, logw, a, g, vg, v_first, k_k, k_a, r_k, lnx_g, lnx_b, bsz, seq):
    n = logw.shape[0]
    d = D_MODEL
    W = WKV_LANES
    rows = min(256, seq)
    n_chunks = rows // WKV_CHUNK
    nt = seq // rows
    nw = d // W
    has_vres = vg is not None

    def tile(off):
        return pl.BlockSpec((rows, W), functools.partial(lambda b, w, t, off: (b * nt + t, off + w), off=off))

    vec = pl.BlockSpec((1, W), lambda b, w, t: (0, w))
    in_specs = [tile(0), tile(nw), tile(2 * nw), tile(0), tile(0), tile(0)]
    args = [rkv, rkv, rkv, logw, a, g]
    if has_vres:
        in_specs += [tile(0), tile(2 * nw)]
        args += [vg, v_first]
    in_specs += [vec] * 5
    args += [z.reshape(1, d) for z in (k_k, k_a, r_k, lnx_g, lnx_b)]
    return pl.pallas_call(
        functools.partial(_wkv_kernel, n_chunks=n_chunks, has_vres=has_vres),
        grid=(bsz, nw, nt),
        in_specs=in_specs,
        out_specs=tile(0),
        out_shape=jax.ShapeDtypeStruct((n, d), BF16),
        scratch_shapes=[pltpu.VMEM((W, W), F32)],
        compiler_params=_cparams("arbitrary", "arbitrary", "arbitrary"),
        name="rwkv7_chunked_state",
    )(*args)


def _route_kernel(x_ref, g_ref, sh_ref, sc_ref, rw_ref, rb_ref, h_ref, eid_ref, gate_ref, rank_ref, cnt_ref,
                  base_ref, before_ref, *, tm):
    step = pl.program_id(0)

    @pl.when(step == 0)
    def _():
        base_ref[...] = jnp.zeros_like(base_ref)
        n0 = lax.broadcasted_iota(jnp.int32, (tm, tm), 0)
        n1 = lax.broadcasted_iota(jnp.int32, (tm, tm), 1)
        before_ref[...] = jnp.where(n0 < n1, 1.0, 0.0).astype(BF16)

    h = _modnorm(x_ref[...], g_ref[...], sc_ref[...], sh_ref[...])
    h_ref[...] = h
    logits = lax.dot_general(rw_ref[...], h, (((1,), (1,)), ((), ())), preferred_element_type=F32,
                             precision=HIGHEST)
    scores = jax.nn.sigmoid(logits)
    sel = scores + rb_ref[...]

    def rows_of(z, grp):
        return [z[grp * EXPERTS_PER_GROUP + j:grp * EXPERTS_PER_GROUP + j + 1, :] for j in range(EXPERTS_PER_GROUP)]

    group_score = []
    for grp in range(N_GROUPS):
        a, b, c, d = rows_of(sel, grp)
        hi1, lo1 = jnp.maximum(a, b), jnp.minimum(a, b)
        hi2, lo2 = jnp.maximum(c, d), jnp.minimum(c, d)
        group_score.append(jnp.maximum(hi1, hi2) + jnp.maximum(jnp.minimum(hi1, hi2), jnp.maximum(lo1, lo2)))
    grp_idx = jnp.zeros((1, tm), jnp.int32)
    best = group_score[0]
    for grp in range(1, N_GROUPS):
        upd = group_score[grp] > best
        grp_idx = jnp.where(upd, grp, grp_idx)
        best = jnp.where(upd, group_score[grp], best)

    def pick_group(z):
        out = rows_of(z, 0)
        for grp in range(1, N_GROUPS):
            cand = rows_of(z, grp)
            out = [jnp.where(grp_idx == grp, cand[j], out[j]) for j in range(EXPERTS_PER_GROUP)]
        return out

    sel_g = pick_group(sel)
    score_g = pick_group(scores)

    def argmax_first(vals):
        idx = jnp.zeros((1, tm), jnp.int32)
        top = vals[0]
        for j in range(1, EXPERTS_PER_GROUP):
            upd = vals[j] > top
            idx = jnp.where(upd, j, idx)
            top = jnp.where(upd, vals[j], top)
        return idx

    loc1 = argmax_first(sel_g)
    loc2 = argmax_first([jnp.where(loc1 == j, -jnp.inf, sel_g[j]) for j in range(EXPERTS_PER_GROUP)])

    def pick_local(vals, loc):
        out = vals[0]
        for j in range(1, EXPERTS_PER_GROUP):
            out = jnp.where(loc == j, vals[j], out)
        return out

    g1 = pick_local(score_g, loc1)
    g2 = pick_local(score_g, loc2)
    gsum = g1 + g2
    e1 = grp_idx * EXPERTS_PER_GROUP + loc1
    e2 = grp_idx * EXPERTS_PER_GROUP + loc2
    eid_ref[...] = jnp.concatenate([e1, e2], axis=0)
    gate_ref[...] = jnp.concatenate([g1 / gsum, g2 / gsum], axis=0)

    expert_row = lax.broadcasted_iota(jnp.int32, (N_EXPERTS, tm), 0)
    member = jnp.logical_or(expert_row == e1, expert_row == e2)
    earlier = base_ref[...] + _dot(jnp.where(member, 1.0, 0.0).astype(BF16), before_ref[...])
    r1 = jnp.sum(jnp.where(expert_row == e1, earlier, 0.0), axis=0, keepdims=True)
    r2 = jnp.sum(jnp.where(expert_row == e2, earlier, 0.0), axis=0, keepdims=True)
    rank_ref[...] = jnp.concatenate([r1, r2], axis=0).astype(jnp.int32)
    base_ref[...] = base_ref[...] + jnp.sum(jnp.where(member, 1.0, 0.0), axis=1, keepdims=True)
    cnt_ref[...] = jnp.broadcast_to(base_ref[...], cnt_ref.shape).astype(jnp.int32)


def _route(x, g, mod_l, router_w, router_bias, seq):
    n, d = x.shape
    tm = min(512, seq)
    tps = seq // tm
    lane_tile = lambda i: (0, i)
    return pl.pallas_call(
        functools.partial(_route_kernel, tm=tm),
        grid=(n // tm,),
        in_specs=[pl.BlockSpec((tm, d), lambda i: (i, 0)),
                  pl.BlockSpec((1, d), lambda i: (0, 0)),
                  _mod_spec(3, tps), _mod_spec(4, tps),
                  pl.BlockSpec((N_EXPERTS, d), lambda i: (0, 0)),
                  pl.BlockSpec((N_EXPERTS, 1), lambda i: (0, 0))],
        out_specs=[pl.BlockSpec((tm, d), lambda i: (i, 0)),
                   pl.BlockSpec((2, tm), lane_tile), pl.BlockSpec((2, tm), lane_tile),
                   pl.BlockSpec((2, tm), lane_tile),
                   pl.BlockSpec((N_EXPERTS, LANES), lambda i: (0, 0))],
        out_shape=[jax.ShapeDtypeStruct((n, d), F32),
                   jax.ShapeDtypeStruct((2, n), jnp.int32), jax.ShapeDtypeStruct((2, n), F32),
                   jax.ShapeDtypeStruct((2, n), jnp.int32),
                   jax.ShapeDtypeStruct((N_EXPERTS, LANES), jnp.int32)],
        scratch_shapes=[pltpu.VMEM((N_EXPERTS, 1), F32), pltpu.VMEM((tm, tm), BF16)],
        compiler_params=_cparams("arbitrary"),
        name="moe_router",
    )(x, g.reshape(1, d), mod_l, mod_l, router_w.T, router_bias.reshape(N_EXPERTS, 1))


def _dispatch_kernel(eid_ref, rank_ref, start_ref, h_hbm, xs_in_hbm, xs_hbm, sem, *, tm):
    del xs_in_hbm
    base = pl.program_id(0) * tm

    def row_copy(t, kk):
        slot = start_ref[eid_ref[kk, t]] + rank_ref[kk, t]
        return pltpu.make_async_copy(h_hbm.at[pl.ds(base + t, 1)], xs_hbm.at[pl.ds(slot, 1)], sem)

    def issue(t, c):
        row_copy(t, 0).start()
        row_copy(t, 1).start()
        return c

    def drain(t, c):
        row_copy(t, 0).wait()
        row_copy(t, 1).wait()
        return c

    lax.fori_loop(0, tm, issue, 0)
    lax.fori_loop(0, tm, drain, 0)


def _dispatch(h, eid, rank, pad_starts, n_slots):
    n, d = h.shape
    tm = min(1024, n)
    smem_tile = pl.BlockSpec((2, tm), lambda i: (0, i), memory_space=pltpu.SMEM)
    return pl.pallas_call(
        functools.partial(_dispatch_kernel, tm=tm),
        grid=(n // tm,),
        in_specs=[smem_tile, smem_tile,
                  pl.BlockSpec(memory_space=pltpu.SMEM),
                  pl.BlockSpec(memory_space=pl.ANY), pl.BlockSpec(memory_space=pl.ANY)],
        out_specs=pl.BlockSpec(memory_space=pl.ANY),
        out_shape=jax.ShapeDtypeStruct((n_slots, d), F32),
        scratch_shapes=[pltpu.SemaphoreType.DMA(())],
        input_output_aliases={4: 0},
        compiler_params=_cparams("arbitrary"),
        name="moe_dispatch",
    )(eid, rank, pad_starts, h, jnp.zeros((n_slots, d), F32))


def _expert_kernel(be_ref, nused_ref, xs_ref, wg_ref, wu_ref, wd_ref, y_ref):
    del be_ref

    @pl.when(pl.program_id(0) < nused_ref[0])
    def _():
        x = xs_ref[...].astype(BF16)
        gt = _dot(x, wg_ref[...])
        up = _dot(x, wu_ref[...])
        mid = (gt * jax.nn.sigmoid(gt) * up).astype(BF16)
        y_ref[...] = _dot(mid, wd_ref[...])

    @pl.when(pl.program_id(0) >= nused_ref[0])
    def _():
        y_ref[...] = jnp.zeros_like(y_ref)


def _experts(xs, block_expert, n_used, w_gate, w_up, w_down):
    n_slots, d = xs.shape
    de = w_gate.shape[-1]
    nb = n_slots // MOE_ROWS
    grid_spec = pltpu.PrefetchScalarGridSpec(
        num_scalar_prefetch=2,
        grid=(nb,),
        in_specs=[pl.BlockSpec((MOE_ROWS, d), lambda b, be, nu: (b, 0)),
                  pl.BlockSpec((None, d, de), lambda b, be, nu: (be[b], 0, 0)),
                  pl.BlockSpec((None, d, de), lambda b, be, nu: (be[b], 0, 0)),
                  pl.BlockSpec((None, de, d), lambda b, be, nu: (be[b], 0, 0))],
        out_specs=pl.BlockSpec((MOE_ROWS, d), lambda b, be, nu: (b, 0)),
    )
    return pl.pallas_call(
        _expert_kernel,
        grid_spec=grid_spec,
        out_shape=jax.ShapeDtypeStruct((n_slots, d), F32),
        compiler_params=_cparams("arbitrary"),
        name="moe_experts",
    )(block_expert, n_used, xs, w_gate, w_up, w_down)


def _combine_kernel(eid_ref, rank_ref, start_ref, gate_ref, x_ref, g2_ref, fin_ref, yb_hbm, o_ref, buf_ref, sem,
                    *, tm, final_norm):
    def row_copy(t, kk):
        slot = start_ref[eid_ref[kk, t]] + rank_ref[kk, t]
        return pltpu.make_async_copy(yb_hbm.at[pl.ds(slot, 1)], buf_ref.at[kk, pl.ds(t, 1)], sem)

    def issue(t, c):
        row_copy(t, 0).start()
        row_copy(t, 1).start()
        return c

    def drain(t, c):
        row_copy(t, 0).wait()
        row_copy(t, 1).wait()
        return c

    lax.fori_loop(0, tm, issue, 0)
    eye = lax.broadcasted_iota(jnp.int32, (tm, tm), 0) == lax.broadcasted_iota(jnp.int32, (tm, tm), 1)
    gates = gate_ref[...]
    w0 = jnp.sum(jnp.where(eye, gates[0:1, :], 0.0), axis=1, keepdims=True)
    w1 = jnp.sum(jnp.where(eye, gates[1:2, :], 0.0), axis=1, keepdims=True)
    lax.fori_loop(0, tm, drain, 0)
    y = w0 * buf_ref[0] + w1 * buf_ref[1]
    out = x_ref[...] + g2_ref[...] * y
    if final_norm:
        out = out * lax.rsqrt(jnp.mean(out * out, axis=-1, keepdims=True) + NORM_EPS) * fin_ref[...]
    o_ref[...] = out


def _combine(yb, eid, rank, gate, pad_starts, x, mod_l, final_g, final_norm, seq):
    n, d = x.shape
    tm = min(256, seq)
    tps = seq // tm
    smem_tile = pl.BlockSpec((2, tm), lambda i: (0, i), memory_space=pltpu.SMEM)
    return pl.pallas_call(
        functools.partial(_combine_kernel, tm=tm, final_norm=final_norm),
        grid=(n // tm,),
        in_specs=[smem_tile, smem_tile,
                  pl.BlockSpec(memory_space=pltpu.SMEM),
                  pl.BlockSpec((2, tm), lambda i: (0, i)),
                  pl.BlockSpec((tm, d), lambda i: (i, 0)),
                  _mod_spec(5, tps),
                  pl.BlockSpec((1, d), lambda i: (0, 0)),
                  pl.BlockSpec(memory_space=pl.ANY)],
        out_specs=pl.BlockSpec((tm, d), lambda i: (i, 0)),
        out_shape=jax.ShapeDtypeStruct((n, d), F32),
        scratch_shapes=[pltpu.VMEM((2, tm, d), F32), pltpu.SemaphoreType.DMA(())],
        compiler_params=_cparams("arbitrary"),
        name="moe_combine",
    )(eid, rank, pad_starts, gate, x, mod_l, final_g.reshape(1, d), yb)


def _grouped_moe_residual(x, g, mod_l, router_w, router_bias, w_gate, w_up, w_down, final_g, final_norm, seq):
    n, d = x.shape
    h, eid, gate, rank, counts = _route(x, g, mod_l, router_w, router_bias, seq)
    counts = counts[:, 0]
    padded = (counts + MOE_ROWS - 1) // MOE_ROWS * MOE_ROWS
    pad_ends = jnp.cumsum(padded)
    pad_starts = (pad_ends - padded).astype(jnp.int32)
    n_slots = 2 * n + N_EXPERTS * MOE_ROWS
    nb = n_slots // MOE_ROWS
    block_start = jnp.arange(nb, dtype=jnp.int32) * MOE_ROWS
    block_expert = jnp.minimum(jnp.searchsorted(pad_ends, block_start, side='right'), N_EXPERTS - 1).astype(jnp.int32)
    n_used = (pad_ends[-1:] // MOE_ROWS).astype(jnp.int32)
    xs = _dispatch(h, eid, rank, pad_starts, n_slots)
    yb = _experts(xs, block_expert, n_used, w_gate, w_up, w_down)
    return _combine(yb, eid, rank, gate, pad_starts, x, mod_l, final_g, final_norm, seq)


def kernel(x, c, w_mod, b_mod, norm1_g, norm2_g, final_g, attn_w_in, attn_w_out, diff_lambda, diff_subln_g, rwkv_mix, rwkv_w_rkv, rwkv_w_out, rwkv_w0, rwkv_w_lora_a, rwkv_w_lora_b, rwkv_a0, rwkv_a_lora_a, rwkv_a_lora_b, rwkv_g_lora_a, rwkv_g_lora_b, rwkv_v0, rwkv_v_lora_a, rwkv_v_lora_b, rwkv_k_k, rwkv_k_a, rwkv_r_k, rwkv_lnx_g, rwkv_lnx_b, router_w, router_bias, moe_w_gate, moe_w_up, moe_w_down):
    bsz, seq, d = x.shape
    depth = w_mod.shape[0]
    n = bsz * seq
    mod = _modulation(c.astype(F32), w_mod, b_mod)
    xf = x.astype(F32).reshape(n, d)
    v_first = None
    for l in range(depth):
        mod_l = mod[l]
        if l % 2 == 0:
            e = l // 2
            lambda_init = 0.8 - 0.6 * math.exp(-0.3 * l)
            proj = _norm_matmul(xf, norm1_g[l], mod_l, attn_w_in[e].astype(BF16), seq)
            o_a = _sb_attention(proj, bsz, seq)
            o_b = _diff_attention(proj, diff_lambda[e], diff_subln_g[e], lambda_init, bsz, seq)
            xf = _out_proj([o_a, o_b], attn_w_out[e].astype(BF16), xf, mod_l, 2, seq)
        else:
            o = l // 2
            v_res = None if o == 0 else (rwkv_v0[o - 1], rwkv_v_lora_a[o - 1], rwkv_v_lora_b[o - 1])
            outs = _rwkv_proj(xf, norm1_g[l], mod_l, rwkv_mix[o], rwkv_w_rkv[o], rwkv_w0[o], rwkv_w_lora_a[o],
                              rwkv_w_lora_b[o], rwkv_a0[o], rwkv_a_lora_a[o], rwkv_a_lora_b[o], rwkv_g_lora_a[o],
                              rwkv_g_lora_b[o], v_res, seq)
            rkv, logw, a, g = outs[:4]
            vg = outs[4] if v_res is not None else None
            if v_res is None:
                v_first = rkv
            yg = _wkv(rkv, logw, a, g, vg, v_first, rwkv_k_k[o], rwkv_k_a[o], rwkv_r_k[o].reshape(d),
                      rwkv_lnx_g[o], rwkv_lnx_b[o], bsz, seq)
            xf = _out_proj([yg], rwkv_w_out[o].astype(BF16), xf, mod_l, 2, seq)
        xf = _grouped_moe_residual(xf, norm2_g[l], mod_l, router_w, router_bias, moe_w_gate[l].astype(BF16),
                                   moe_w_up[l].astype(BF16), moe_w_down[l].astype(BF16), final_g,
                                   l == depth - 1, seq)
    return xf.reshape(bsz, seq, d)
```

```python
import functools
import math

import jax
import jax.numpy as jnp
from jax import lax
from jax.experimental import pallas as pl
from jax.experimental.pallas import tpu as pltpu

F32 = jnp.float32
BF16 = jnp.bfloat16
HIGHEST = lax.Precision.HIGHEST

D_MODEL = 2048
SB_HEADS = 8
SB_HEAD_DIM = 128
DIFF_HEADS = 8
DIFF_QK_DIM = 64
DIFF_V_DIM = 128
SB_WIDTH = SB_HEADS * SB_HEAD_DIM
DIFF_QK_WIDTH = DIFF_HEADS * 2 * DIFF_QK_DIM
DIFF_V_WIDTH = DIFF_HEADS * DIFF_V_DIM
PROJ_WIDTH = 3 * SB_WIDTH + 2 * DIFF_QK_WIDTH + DIFF_V_WIDTH
RWKV_HEAD_DIM = 64
RWKV_GN_EPS = 64e-5
N_EXPERTS = 16
N_GROUPS = 4
EXPERTS_PER_GROUP = 4
D_EXPERT = D_MODEL // 2
NORM_EPS = 1e-6

LANES = 128
WKV_CHUNK = 64
WKV_LANES = 256
WKV_HEADS = WKV_LANES // RWKV_HEAD_DIM
MOE_ROWS = 256
NEG_BIG = -1e30
VMEM_LIMIT = 56 * 1024 * 1024


def _cparams(*sem):
    return pltpu.CompilerParams(dimension_semantics=sem, vmem_limit_bytes=VMEM_LIMIT)


def _dot(a, b):
    return jnp.dot(a, b, preferred_element_type=F32)


def _dot_nt(a, b):
    return lax.dot_general(a, b, (((1,), (1,)), ((), ())), preferred_element_type=F32)


def _dot_tn(a, b):
    return lax.dot_general(a, b, (((0,), (0,)), ((), ())), preferred_element_type=F32)


def _softplus(z):
    return jnp.maximum(z, 0.0) + jnp.log1p(jnp.exp(-jnp.abs(z)))


def _modnorm(x, g, scale, shift):
    ms = jnp.mean(x * x, axis=-1, keepdims=True)
    return x * lax.rsqrt(ms + NORM_EPS) * g * (1.0 + scale) + shift


def _mod_spec(which, rows_per_batch_tile):
    return pl.BlockSpec((None, None, 1, D_MODEL), lambda i, *_: (i // rows_per_batch_tile, which, 0, 0))


def _mod_kernel(c_ref, w_ref, b_ref, o_ref):
    c = c_ref[...]
    cond = c * jax.nn.sigmoid(c)
    o_ref[...] = jnp.dot(cond, w_ref[...], preferred_element_type=F32, precision=HIGHEST) + b_ref[...]


def _modulation(c, w_mod, b_mod):
    depth, d, n6 = w_mod.shape
    bsz = c.shape[0]
    tn = 1536
    out = pl.pallas_call(
        _mod_kernel,
        grid=(depth, n6 // tn),
        in_specs=[pl.BlockSpec((bsz, d), lambda l, j: (0, 0)),
                  pl.BlockSpec((None, d, tn), lambda l, j: (l, 0, j)),
                  pl.BlockSpec((None, 1, tn), lambda l, j: (l, 0, j))],
        out_specs=pl.BlockSpec((None, bsz, tn), lambda l, j: (l, 0, j)),
        out_shape=jax.ShapeDtypeStruct((depth, bsz, n6), F32),
        compiler_params=_cparams("arbitrary", "arbitrary"),
        name="adaln_modulation",
    )(c, w_mod, b_mod.reshape(depth, 1, n6))
    return out.reshape(depth, bsz, 6, 1, d)


def _normmm_kernel(x_ref, g_ref, sh_ref, sc_ref, w_ref, o_ref, h_ref):
    @pl.when(pl.program_id(1) == 0)
    def _():
        h_ref[...] = _modnorm(x_ref[...], g_ref[...], sc_ref[...], sh_ref[...]).astype(BF16)

    o_ref[...] = _dot(h_ref[...], w_ref[...]).astype(o_ref.dtype)


def _norm_matmul(x, g, mod_l, w, seq):
    n, d = x.shape
    nout = w.shape[1]
    tm = min(1024, seq)
    tn = 512
    tps = seq // tm
    return pl.pallas_call(
        _normmm_kernel,
        grid=(n // tm, nout // tn),
        in_specs=[pl.BlockSpec((tm, d), lambda i, j: (i, 0)),
                  pl.BlockSpec((1, d), lambda i, j: (0, 0)),
                  _mod_spec(0, tps), _mod_spec(1, tps),
                  pl.BlockSpec((d, tn), lambda i, j: (0, j))],
        out_specs=pl.BlockSpec((tm, tn), lambda i, j: (i, j)),
        out_shape=jax.ShapeDtypeStruct((n, nout), BF16),
        scratch_shapes=[pltpu.VMEM((tm, d), BF16)],
        compiler_params=_cparams("arbitrary", "arbitrary"),
        name="attn_in_proj",
    )(x, g.reshape(1, d), mod_l, mod_l, w)


def _outproj_kernel(*refs, n_lhs):
    lhs = refs[:n_lhs]
    ws = refs[n_lhs:2 * n_lhs]
    x_ref, gate_ref, o_ref = refs[2 * n_lhs:]
    acc = _dot(lhs[0][...], ws[0][...])
    for a_ref, w_ref in zip(lhs[1:], ws[1:]):
        acc = acc + _dot(a_ref[...], w_ref[...])
    o_ref[...] = x_ref[...] + gate_ref[...] * acc


def _out_proj(lhs_list, w, x, mod_l, gate_idx, seq):
    n, d = x.shape
    tm = min(1024, seq)
    tn = 512
    tps = seq // tm
    n_lhs = len(lhs_list)
    kp = w.shape[0] // n_lhs
    in_specs = [pl.BlockSpec((tm, kp), lambda i, j: (i, 0)) for _ in lhs_list]
    in_specs += [pl.BlockSpec((kp, tn), functools.partial(lambda i, j, p: (p, j), p=p)) for p in range(n_lhs)]
    in_specs += [pl.BlockSpec((tm, tn), lambda i, j: (i, j)),
                 pl.BlockSpec((None, None, 1, tn), lambda i, j: (i // tps, gate_idx, 0, j))]
    return pl.pallas_call(
        functools.partial(_outproj_kernel, n_lhs=n_lhs),
        grid=(n // tm, d // tn),
        in_specs=in_specs,
        out_specs=pl.BlockSpec((tm, tn), lambda i, j: (i, j)),
        out_shape=jax.ShapeDtypeStruct((n, d), F32),
        compiler_params=_cparams("arbitrary", "arbitrary"),
        name="out_proj_residual",
    )(*lhs_list, *([w] * n_lhs), x, mod_l)


def _sb_kernel(q_ref, k_ref, v_ref, o_ref, *, tq, tk, scale):
    qi = pl.program_id(2)
    q = q_ref[...]
    jj = lax.broadcasted_iota(jnp.int32, (tk, tk), 0)
    ss = lax.broadcasted_iota(jnp.int32, (tk, tk), 1)
    later = jnp.where(jj > ss, 1.0, 0.0).astype(BF16)
    rel0 = (lax.broadcasted_iota(jnp.int32, (tq, tk), 0) - lax.broadcasted_iota(jnp.int32, (tq, tk), 1))

    def block(kb, carry, acc, masked):
        start = pl.multiple_of(kb * tk, tk)
        k = k_ref[pl.ds(start, tk), :]
        v = v_ref[pl.ds(start, tk), :]
        z = _dot_nt(q, k) * scale
        sp = _softplus(z)
        log_keep = -sp
        if masked:
            strict = (rel0 + (qi * tq - kb * tk)) > 0
            log_keep = jnp.where(strict, log_keep, 0.0)
        hi = log_keep.astype(BF16)
        lo = (log_keep - hi.astype(F32)).astype(BF16)
        cs = _dot(jnp.concatenate([hi, lo], axis=0), later)
        log_between = cs[:tq] + cs[tq:] + carry
        w = jnp.exp(z - sp + log_between)
        if masked:
            w = jnp.where(strict, w, 0.0)
        acc = acc + _dot(w.astype(BF16), v)
        carry = carry + jnp.sum(log_keep, axis=1, keepdims=True)
        return carry, acc

    carry = jnp.zeros((tq, 1), F32)
    acc = jnp.zeros((tq, SB_HEAD_DIM), F32)
    n_full = (qi * tq) // tk
    last = ((qi + 1) * tq - 1) // tk
    for m in range(max(tq // tk, 1)):
        carry, acc = block(last - m, carry, acc, True)

    def body(i, c):
        return block(n_full - 1 - i, c[0], c[1], False)

    carry, acc = lax.fori_loop(0, n_full, body, (carry, acc))
    o_ref[...] = acc.astype(o_ref.dtype)


def _sb_attention(proj, bsz, seq):
    n = proj.shape[0]
    tq = min(256, seq)
    tk = min(128, seq)
    nq = seq // tq
    kern = functools.partial(_sb_kernel, tq=tq, tk=tk, scale=SB_HEAD_DIM ** -0.5)
    return pl.pallas_call(
        kern,
        grid=(bsz, SB_HEADS, nq),
        in_specs=[pl.BlockSpec((tq, LANES), lambda b, h, i: (b * nq + i, h)),
                  pl.BlockSpec((seq, LANES), lambda b, h, i: (b, SB_HEADS + h)),
                  pl.BlockSpec((seq, LANES), lambda b, h, i: (b, 2 * SB_HEADS + h))],
        out_specs=pl.BlockSpec((tq, LANES), lambda b, h, i: (b * nq + i, h)),
        out_shape=jax.ShapeDtypeStruct((n, SB_WIDTH), BF16),
        compiler_params=_cparams("arbitrary", "arbitrary", "arbitrary"),
        name="stick_breaking_attention",
    )(proj, proj, proj)


def _diff_kernel(q_ref, k_ref, v_ref, slope_ref, lam_ref, g_ref, o_ref, *, tq, tk, scale, lambda_init):
    qi = pl.program_id(2)
    q = q_ref[...]
    lane = lax.broadcasted_iota(jnp.int32, (tq, LANES), 1)
    zero = jnp.zeros_like(q)
    q2 = jnp.concatenate([jnp.where(lane < DIFF_QK_DIM, q, zero), jnp.where(lane >= DIFF_QK_DIM, q, zero)], axis=0)
    row = lax.broadcasted_iota(jnp.int32, (2 * tq, tk), 0)
    rel0 = jnp.where(row >= tq, row - tq, row) - lax.broadcasted_iota(jnp.int32, (2 * tq, tk), 1)
    slope = slope_ref[...]

    def block(kb, m, l, acc, masked):
        start = pl.multiple_of(kb * tk, tk)
        k = k_ref[pl.ds(start, tk), :]
        v = v_ref[pl.ds(start, tk), :]
        rel = rel0 + (qi * tq - kb * tk)
        s = _dot_nt(q2, k) * scale - slope * rel.astype(F32)
        if masked:
            s = jnp.where(rel >= 0, s, NEG_BIG)
        m_new = jnp.maximum(m, jnp.max(s, axis=1, keepdims=True))
        alpha = jnp.exp(m - m_new)
        p = jnp.exp(s - m_new)
        l = alpha * l + jnp.sum(p, axis=1, keepdims=True)
        acc = alpha * acc + _dot(p.astype(BF16), v)
        return m_new, l, acc

    m = jnp.full((2 * tq, 1), NEG_BIG, F32)
    l = jnp.zeros((2 * tq, 1), F32)
    acc = jnp.zeros((2 * tq, DIFF_V_DIM), F32)
    n_full = (qi * tq) // tk
    last = ((qi + 1) * tq - 1) // tk
    m, l, acc = lax.fori_loop(0, n_full, lambda kb, c: block(kb, c[0], c[1], c[2], False), (m, l, acc))
    n_masked = max(tq // tk, 1)
    for j in range(n_masked):
        m, l, acc = block(last - (n_masked - 1 - j), m, l, acc, True)

    lam4 = lam_ref[...]
    lam = (jnp.exp(jnp.sum(lam4[0:1] * lam4[1:2], axis=1, keepdims=True))
           - jnp.exp(jnp.sum(lam4[2:3] * lam4[3:4], axis=1, keepdims=True)) + lambda_init)
    o = acc[:tq] / l[:tq] - lam * (acc[tq:] / l[tq:])
    o = o * lax.rsqrt(jnp.mean(o * o, axis=-1, keepdims=True) + NORM_EPS) * g_ref[...] * (1.0 - lambda_init)
    o_ref[...] = o.astype(o_ref.dtype)


def _diff_attention(proj, diff_lambda, subln_g, lambda_init, bsz, seq):
    n = proj.shape[0]
    tq = min(128, seq)
    tk = min(256, seq)
    nq = seq // tq
    qoff = 3 * SB_HEADS
    slopes = 2.0 ** (-8.0 * (jnp.arange(DIFF_HEADS, dtype=F32) + 1.0) / DIFF_HEADS)
    slopes = jnp.broadcast_to(slopes[:, None, None], (DIFF_HEADS, 1, tk))
    kern = functools.partial(_diff_kernel, tq=tq, tk=tk, scale=DIFF_QK_DIM ** -0.5, lambda_init=lambda_init)
    return pl.pallas_call(
        kern,
        grid=(bsz, DIFF_HEADS, nq),
        in_specs=[pl.BlockSpec((tq, LANES), lambda b, h, i: (b * nq + i, qoff + h)),
                  pl.BlockSpec((seq, LANES), lambda b, h, i: (b, qoff + DIFF_HEADS + h)),
                  pl.BlockSpec((seq, LANES), lambda b, h, i: (b, qoff + 2 * DIFF_HEADS + h)),
                  pl.BlockSpec((None, 1, tk), lambda b, h, i: (h, 0, 0)),
                  pl.BlockSpec((4, DIFF_QK_DIM), lambda b, h, i: (0, 0)),
                  pl.BlockSpec((1, DIFF_V_DIM), lambda b, h, i: (0, 0))],
        out_specs=pl.BlockSpec((tq, LANES), lambda b, h, i: (b * nq + i, h)),
        out_shape=jax.ShapeDtypeStruct((n, DIFF_V_WIDTH), BF16),
        compiler_params=_cparams("arbitrary", "arbitrary", "arbitrary"),
        name="differential_attention",
    )(proj, proj, proj, slopes, diff_lambda, subln_g.reshape(1, DIFF_V_DIM))


RKV_TILES = 12
LORA_TILES = 4
RWKV_TN = 512


def _rwkv_proj_kernel(*refs, tm, tiles_per_seq, has_vres):
    (x_ref, xp_ref, g_ref, sh_ref, sc_ref, mix_ref, w_ref, wla_ref, ala_ref, gla_ref,
     wlb_ref, alb_ref, glb_ref, w0_ref, a0_ref) = refs[:15]
    pos = 15
    if has_vres:
        vla_ref, vlb_ref, v0_ref = refs[pos:pos + 3]
        pos += 3
    rkv_ref, logw_ref, a_out_ref, g_out_ref = refs[pos:pos + 4]
    pos += 4
    if has_vres:
        vg_out_ref = refs[pos]
        pos += 1
    lerp_ref, l1w_ref, l1a_ref, l1g_ref = refs[pos:pos + 4]
    pos += 4
    if has_vres:
        l1v_ref = refs[pos]

    i = pl.program_id(0)
    j = pl.program_id(1)

    @pl.when(j == 0)
    def _():
        g, sc, sh = g_ref[...], sc_ref[...], sh_ref[...]
        h = _modnorm(x_ref[...], g, sc, sh)
        hp = _modnorm(xp_ref[...], g, sc, sh)[7:8, :]
        hp = jnp.where(i % tiles_per_seq == 0, jnp.zeros_like(hp), hp)
        first_row = lax.broadcasted_iota(jnp.int32, h.shape, 0) == 0
        dx = jnp.where(first_row, hp, pltpu.roll(h, 1, 0)) - h
        for m in range(6):
            lerp_ref[m] = (h + dx * mix_ref[m:m + 1, :]).astype(BF16)

    @pl.when(j < RKV_TILES)
    def _():
        sel = j // (RKV_TILES // 3)
        idx = jnp.where(sel == 0, 0, sel + 1)
        rkv_ref[...] = _dot(lerp_ref[idx], w_ref[...]).astype(rkv_ref.dtype)

    @pl.when(j == RKV_TILES)
    def _():
        l1w_ref[...] = jnp.tanh(_dot(lerp_ref[1], wla_ref[...])).astype(BF16)
        l1a_ref[...] = _dot(lerp_ref[4], ala_ref[...]).astype(BF16)
        l1g_ref[...] = jax.nn.sigmoid(_dot(lerp_ref[5], gla_ref[...])).astype(BF16)
        if has_vres:
            l1v_ref[...] = _dot(lerp_ref[3], vla_ref[...]).astype(BF16)

    @pl.when(j >= RKV_TILES)
    def _():
        pre = w0_ref[...] + _dot(l1w_ref[...], wlb_ref[...])
        logw_ref[...] = -jnp.exp(-_softplus(-pre) - 0.5)
        a_out_ref[...] = jax.nn.sigmoid(a0_ref[...] + _dot(l1a_ref[...], alb_ref[...])).astype(BF16)
        g_out_ref[...] = _dot(l1g_ref[...], glb_ref[...]).astype(BF16)
        if has_vres:
            vg_out_ref[...] = jax.nn.sigmoid(v0_ref[...] + _dot(l1v_ref[...], vlb_ref[...])).astype(BF16)


def _pad_cols(w, to):
    return jnp.pad(w, ((0, 0), (0, to - w.shape[1])))


def _pad_rows(w, to):
    return jnp.pad(w, ((0, to - w.shape[0]), (0, 0)))


def _rwkv_proj(x, g, mod_l, mix, w_rkv, w0, w_la, w_lb, a0, a_la, a_lb, g_la, g_lb, v_res, seq):
    n, d = x.shape
    tm = min(512, seq)
    tn = RWKV_TN
    tps = seq // tm
    has_vres = v_res is not None
    wcat = jnp.concatenate([w_rkv[0], w_rkv[1], w_rkv[2]], axis=1).astype(BF16)
    lw = LANES
    gl = g_la.shape[1]

    def rkv_col(i, j):
        return (0, jnp.minimum(j, RKV_TILES - 1))

    def lora_col(i, j):
        return (0, jnp.clip(j - RKV_TILES, 0, LORA_TILES - 1))

    def full(shape):
        return pl.BlockSpec(shape, lambda i, j: (0, 0))

    in_specs = [pl.BlockSpec((tm, d), lambda i, j: (i, 0)),
                pl.BlockSpec((8, d), lambda i, j: (jnp.maximum(i * (tm // 8) - 1, 0), 0)),
                full((1, d)), _mod_spec(0, tps), _mod_spec(1, tps), full((6, d)),
                pl.BlockSpec((d, tn), rkv_col),
                full((d, lw)), full((d, lw)), full((d, gl)),
                pl.BlockSpec((lw, tn), lora_col), pl.BlockSpec((lw, tn), lora_col), pl.BlockSpec((gl, tn), lora_col),
                pl.BlockSpec((1, tn), lora_col), pl.BlockSpec((1, tn), lora_col)]
    args = [x, x, g.reshape(1, d), mod_l, mod_l, mix, wcat,
            _pad_cols(w_la, lw).astype(BF16), _pad_cols(a_la, lw).astype(BF16), g_la.astype(BF16),
            _pad_rows(w_lb, lw).astype(BF16), _pad_rows(a_lb, lw).astype(BF16), g_lb.astype(BF16),
            w0.reshape(1, d), a0.reshape(1, d)]
    out_tile = lambda i, j: (i, jnp.clip(j - RKV_TILES, 0, LORA_TILES - 1))
    out_specs = [pl.BlockSpec((tm, tn), lambda i, j: (i, jnp.minimum(j, RKV_TILES - 1))),
                 pl.BlockSpec((tm, tn), out_tile), pl.BlockSpec((tm, tn), out_tile), pl.BlockSpec((tm, tn), out_tile)]
    out_shape = [jax.ShapeDtypeStruct((n, 3 * d), BF16), jax.ShapeDtypeStruct((n, d), F32),
                 jax.ShapeDtypeStruct((n, d), BF16), jax.ShapeDtypeStruct((n, d), BF16)]
    scratch = [pltpu.VMEM((6, tm, d), BF16), pltpu.VMEM((tm, lw), BF16), pltpu.VMEM((tm, lw), BF16),
               pltpu.VMEM((tm, gl), BF16)]
    if has_vres:
        v0, v_la, v_lb = v_res
        in_specs += [full((d, lw)), pl.BlockSpec((lw, tn), lora_col), pl.BlockSpec((1, tn), lora_col)]
        args += [_pad_cols(v_la, lw).astype(BF16), _pad_rows(v_lb, lw).astype(BF16), v0.reshape(1, d)]
        out_specs.append(pl.BlockSpec((tm, tn), out_tile))
        out_shape.append(jax.ShapeDtypeStruct((n, d), BF16))
        scratch.append(pltpu.VMEM((tm, lw), BF16))
    return pl.pallas_call(
        functools.partial(_rwkv_proj_kernel, tm=tm, tiles_per_seq=tps, has_vres=has_vres),
        grid=(n // tm, RKV_TILES + LORA_TILES),
        in_specs=in_specs, out_specs=out_specs, out_shape=out_shape, scratch_shapes=scratch,
        compiler_params=_cparams("arbitrary", "arbitrary"),
        name="rwkv_projections",
    )(*args)


def _split3(x):
    hi = x.astype(BF16)
    r1 = x - hi.astype(F32)
    mid = r1.astype(BF16)
    lo = (r1 - mid.astype(F32)).astype(BF16)
    return hi, mid, lo


def _mm_precise(a, b):
    ah = a.astype(BF16)
    al = (a - ah.astype(F32)).astype(BF16)
    bh = b.astype(BF16)
    bl = (b - bh.astype(F32)).astype(BF16)
    return _dot(ah, bh) + _dot(ah, bl) + _dot(al, bh)


def _unit_lower_inverse(a, row, col):
    def same_block(nb):
        return (row // nb) == (col // nb)

    eye = jnp.where(row == col, 1.0, 0.0)
    b16 = same_block(16)
    ad = jnp.where(b16, a, 0.0)
    a2 = _mm_precise(ad, ad)
    a4 = _mm_precise(a2, a2)
    a8 = _mm_precise(a4, a4)
    t = eye + ad + a2 + _mm_precise(ad, a2)
    t = t + _mm_precise(t, a4)
    t = t + _mm_precise(t, a8)
    prev = b16
    for nb in (32, 64):
        cur = same_block(nb)
        off = jnp.where(jnp.logical_and(cur, jnp.logical_not(prev)), a, 0.0)
        t = t + _mm_precise(_mm_precise(t, off), t)
        prev = cur
    return t


def _wkv_kernel(*refs, n_chunks, has_vres):
    (r_ref, k_ref, v_ref, lw_ref, a_ref, g_ref) = refs[:6]
    pos = 6
    if has_vres:
        vg_ref, vf_ref = refs[pos:pos + 2]
        pos += 2
    kk_ref, ka_ref, rk_ref, lng_ref, lnb_ref = refs[pos:pos + 5]
    pos += 5
    y_ref = refs[pos]
    s_ref = refs[pos + 1]

    L, W = WKV_CHUNK, WKV_LANES
    R = WKV_HEADS * L

    @pl.when(pl.program_id(2) == 0)
    def _():
        s_ref[...] = jnp.zeros_like(s_ref)

    lane = lax.broadcasted_iota(jnp.int32, (L, W), 1)
    head_masks = [(lane // RWKV_HEAD_DIM) == h for h in range(WKV_HEADS)]
    row = lax.broadcasted_iota(jnp.int32, (R, R), 0)
    col = lax.broadcasted_iota(jnp.int32, (R, R), 1)
    strict = row > col
    incl = row >= col
    tt = lax.broadcasted_iota(jnp.int32, (L, L), 0)
    tj = lax.broadcasted_iota(jnp.int32, (L, L), 1)
    upto = jnp.where(tj <= tt, 1.0, 0.0).astype(BF16)

    def stack(z):
        return jnp.concatenate([jnp.where(mk, z, 0.0) for mk in head_masks], axis=0)

    def head_sum(z):
        out = jnp.zeros_like(z)
        for mk in head_masks:
            out = jnp.where(mk, jnp.sum(jnp.where(mk, z, 0.0), axis=1, keepdims=True), out)
        return out

    k_k, k_a, r_k = kk_ref[...], ka_ref[...], rk_ref[...]
    ln_g, ln_b = lng_ref[...], lnb_ref[...]

    def chunk(c, carry):
        rows = pl.ds(pl.multiple_of(c * L, L), L)
        r = r_ref[rows, :].astype(F32)
        k = k_ref[rows, :].astype(F32)
        v = v_ref[rows, :].astype(F32)
        a = a_ref[rows, :].astype(F32)
        lw = lw_ref[rows, :]
        if has_vres:
            v = v + (vf_ref[rows, :].astype(F32) - v) * vg_ref[rows, :].astype(F32)
        kk = k * k_k
        kk = kk / jnp.maximum(jnp.sqrt(head_sum(kk * kk)), 1e-12)
        km = k * (1.0 + (a - 1.0) * k_a)
        bv = kk * a

        hi, mid, lo = _split3(lw)
        cs = _dot(upto, jnp.concatenate([hi, mid, lo], axis=1))
        cl = cs[:, :W] + cs[:, W:2 * W] + cs[:, 2 * W:]
        p_in = jnp.exp(cl)
        p_inv = jnp.exp(-cl)
        p_last = p_in[L - 1:L, :]
        at = -kk * jnp.exp(cl - lw)
        rt = r * p_in
        bt = bv * p_inv
        kt = km * p_inv

        s0 = s_ref[...]
        ar = jnp.concatenate([stack(at), stack(rt)], axis=0).astype(BF16)
        bk = jnp.concatenate([stack(bt), stack(kt)], axis=0).astype(BF16)
        gm = _dot_nt(ar, bk)
        a_ab = jnp.where(strict, gm[:R, :R], 0.0)
        a_ak = jnp.where(strict, gm[:R, R:], 0.0)
        a_rb = jnp.where(incl, gm[R:, :R], 0.0)
        a_rk = jnp.where(incl, gm[R:, R:], 0.0)
        ss = _dot_nt(ar, s0.astype(BF16))
        vx = stack(v)
        vxb = vx.astype(BF16)
        x = ss[:R] + _dot(a_ak.astype(BF16), vxb)
        u = _mm_precise(_unit_lower_inverse(a_ab, row, col), x)
        uv = jnp.concatenate([u.astype(BF16), vxb], axis=0)
        yx = ss[R:] + _dot(jnp.concatenate([a_rb, a_rk], axis=1).astype(BF16), uv)
        y = yx[0:L]
        for h in range(1, WKV_HEADS):
            y = y + yx[h * L:(h + 1) * L]
        bkl = jnp.concatenate([stack(bt * p_last), stack(kt * p_last)], axis=0).astype(BF16)
        s_ref[...] = s0 * p_last + _dot_tn(uv, bkl)

        inv_n = 1.0 / RWKV_HEAD_DIM
        mu = head_sum(y) * inv_n
        dlt = y - mu
        var = head_sum(dlt * dlt) * inv_n
        yn = dlt * lax.rsqrt(var + RWKV_GN_EPS) * ln_g + ln_b
        yn = yn + head_sum(r * km * r_k) * v
        y_ref[rows, :] = (yn * g_ref[rows, :].astype(F32)).astype(y_ref.dtype)
        return carry

    lax.fori_loop(0, n_chunks, chunk, 0)


def _wkv(rkv, logw, a, g, vg, v_first, k_k, k_a, r_k, lnx_g, lnx_b, bsz, seq):
    n = logw.shape[0]
    d = D_MODEL
    W = WKV_LANES
    rows = min(256, seq)
    n_chunks = rows // WKV_CHUNK
    nt = seq // rows
    nw = d // W
    has_vres = vg is not None

    def tile(off):
        return pl.BlockSpec((rows, W), functools.partial(lambda b, w, t, off: (b * nt + t, off + w), off=off))

    vec = pl.BlockSpec((1, W), lambda b, w, t: (0, w))
    in_specs = [tile(0), tile(nw), tile(2 * nw), tile(0), tile(0), tile(0)]
    args = [rkv, rkv, rkv, logw, a, g]
    if has_vres:
        in_specs += [tile(0), tile(2 * nw)]
        args += [vg, v_first]
    in_specs += [vec] * 5
    args += [z.reshape(1, d) for z in (k_k, k_a, r_k, lnx_g, lnx_b)]
    return pl.pallas_call(
        functools.partial(_wkv_kernel, n_chunks=n_chunks, has_vres=has_vres),
        grid=(bsz, nw, nt),
        in_specs=in_specs,
        out_specs=tile(0),
        out_shape=jax.ShapeDtypeStruct((n, d), BF16),
        scratch_shapes=[pltpu.VMEM((W, W), F32)],
        compiler_params=_cparams("arbitrary", "arbitrary", "arbitrary"),
        name="rwkv7_chunked_state",
    )(*args)


def _route_kernel(x_ref, g_ref, sh_ref, sc_ref, rw_ref, rb_ref, h_ref, eid_ref, gate_ref, rank_ref, cnt_ref,
                  base_ref, before_ref, *, tm):
    step = pl.program_id(0)

    @pl.when(step == 0)
    def _():
        base_ref[...] = jnp.zeros_like(base_ref)
        n0 = lax.broadcasted_iota(jnp.int32, (tm, tm), 0)
        n1 = lax.broadcasted_iota(jnp.int32, (tm, tm), 1)
        before_ref[...] = jnp.where(n0 < n1, 1.0, 0.0).astype(BF16)

    h = _modnorm(x_ref[...], g_ref[...], sc_ref[...], sh_ref[...])
    h_ref[...] = h
    logits = lax.dot_general(rw_ref[...], h, (((1,), (1,)), ((), ())), preferred_element_type=F32,
                             precision=HIGHEST)
    scores = jax.nn.sigmoid(logits)
    sel = scores + rb_ref[...]

    def rows_of(z, grp):
        return [z[grp * EXPERTS_PER_GROUP + j:grp * EXPERTS_PER_GROUP + j + 1, :] for j in range(EXPERTS_PER_GROUP)]

    group_score = []
    for grp in range(N_GROUPS):
        a, b, c, d = rows_of(sel, grp)
        hi1, lo1 = jnp.maximum(a, b), jnp.minimum(a, b)
        hi2, lo2 = jnp.maximum(c, d), jnp.minimum(c, d)
        group_score.append(jnp.maximum(hi1, hi2) + jnp.maximum(jnp.minimum(hi1, hi2), jnp.maximum(lo1, lo2)))
    grp_idx = jnp.zeros((1, tm), jnp.int32)
    best = group_score[0]
    for grp in range(1, N_GROUPS):
        upd = group_score[grp] > best
        grp_idx = jnp.where(upd, grp, grp_idx)
        best = jnp.where(upd, group_score[grp], best)

    def pick_group(z):
        out = rows_of(z, 0)
        for grp in range(1, N_GROUPS):
            cand = rows_of(z, grp)
            out = [jnp.where(grp_idx == grp, cand[j], out[j]) for j in range(EXPERTS_PER_GROUP)]
        return out

    sel_g = pick_group(sel)
    score_g = pick_group(scores)

    def argmax_first(vals):
        idx = jnp.zeros((1, tm), jnp.int32)
        top = vals[0]
        for j in range(1, EXPERTS_PER_GROUP):
            upd = vals[j] > top
            idx = jnp.where(upd, j, idx)
            top = jnp.where(upd, vals[j], top)
        return idx

    loc1 = argmax_first(sel_g)
    loc2 = argmax_first([jnp.where(loc1 == j, -jnp.inf, sel_g[j]) for j in range(EXPERTS_PER_GROUP)])

    def pick_local(vals, loc):
        out = vals[0]
        for j in range(1, EXPERTS_PER_GROUP):
            out = jnp.where(loc == j, vals[j], out)
        return out

    g1 = pick_local(score_g, loc1)
    g2 = pick_local(score_g, loc2)
    gsum = g1 + g2
    e1 = grp_idx * EXPERTS_PER_GROUP + loc1
    e2 = grp_idx * EXPERTS_PER_GROUP + loc2
    eid_ref[...] = jnp.concatenate([e1, e2], axis=0)
    gate_ref[...] = jnp.concatenate([g1 / gsum, g2 / gsum], axis=0)

    expert_row = lax.broadcasted_iota(jnp.int32, (N_EXPERTS, tm), 0)
    member = jnp.logical_or(expert_row == e1, expert_row == e2)
    earlier = base_ref[...] + _dot(jnp.where(member, 1.0, 0.0).astype(BF16), before_ref[...])
    r1 = jnp.sum(jnp.where(expert_row == e1, earlier, 0.0), axis=0, keepdims=True)
    r2 = jnp.sum(jnp.where(expert_row == e2, earlier, 0.0), axis=0, keepdims=True)
    rank_ref[...] = jnp.concatenate([r1, r2], axis=0).astype(jnp.int32)
    base_ref[...] = base_ref[...] + jnp.sum(jnp.where(member, 1.0, 0.0), axis=1, keepdims=True)
    cnt_ref[...] = jnp.broadcast_to(base_ref[...], cnt_ref.shape).astype(jnp.int32)


def _route(x, g, mod_l, router_w, router_bias, seq):
    n, d = x.shape
    tm = min(512, seq)
    tps = seq // tm
    lane_tile = lambda i: (0, i)
    return pl.pallas_call(
        functools.partial(_route_kernel, tm=tm),
        grid=(n // tm,),
        in_specs=[pl.BlockSpec((tm, d), lambda i: (i, 0)),
                  pl.BlockSpec((1, d), lambda i: (0, 0)),
                  _mod_spec(3, tps), _mod_spec(4, tps),
                  pl.BlockSpec((N_EXPERTS, d), lambda i: (0, 0)),
                  pl.BlockSpec((N_EXPERTS, 1), lambda i: (0, 0))],
        out_specs=[pl.BlockSpec((tm, d), lambda i: (i, 0)),
                   pl.BlockSpec((2, tm), lane_tile), pl.BlockSpec((2, tm), lane_tile),
                   pl.BlockSpec((2, tm), lane_tile),
                   pl.BlockSpec((N_EXPERTS, LANES), lambda i: (0, 0))],
        out_shape=[jax.ShapeDtypeStruct((n, d), F32),
                   jax.ShapeDtypeStruct((2, n), jnp.int32), jax.ShapeDtypeStruct((2, n), F32),
                   jax.ShapeDtypeStruct((2, n), jnp.int32),
                   jax.ShapeDtypeStruct((N_EXPERTS, LANES), jnp.int32)],
        scratch_shapes=[pltpu.VMEM((N_EXPERTS, 1), F32), pltpu.VMEM((tm, tm), BF16)],
        compiler_params=_cparams("arbitrary"),
        name="moe_router",
    )(x, g.reshape(1, d), mod_l, mod_l, router_w.T, router_bias.reshape(N_EXPERTS, 1))


def _dispatch_kernel(eid_ref, rank_ref, start_ref, h_ref, xs_in_hbm, xs_hbm, sem, *, tm):
    del xs_in_hbm

    def row_copy(t, kk):
        slot = start_ref[eid_ref[kk, t]] + rank_ref[kk, t]
        return pltpu.make_async_copy(h_ref.at[pl.ds(t, 1)], xs_hbm.at[pl.ds(slot, 1)], sem)

    def issue(t, c):
        row_copy(t, 0).start()
        row_copy(t, 1).start()
        return c

    def drain(t, c):
        row_copy(t, 0).wait()
        row_copy(t, 1).wait()
        return c

    lax.fori_loop(0, tm, issue, 0)
    lax.fori_loop(0, tm, drain, 0)


def _dispatch(h, eid, rank, pad_starts, n_slots):
    n, d = h.shape
    tm = min(1024, n)
    smem_tile = pl.BlockSpec((2, tm), lambda i: (0, i), memory_space=pltpu.SMEM)
    return pl.pallas_call(
        functools.partial(_dispatch_kernel, tm=tm),
        grid=(n // tm,),
        in_specs=[smem_tile, smem_tile,
                  pl.BlockSpec(memory_space=pltpu.SMEM),
                  pl.BlockSpec((tm, d), lambda i: (i, 0)), pl.BlockSpec(memory_space=pl.ANY)],
        out_specs=pl.BlockSpec(memory_space=pl.ANY),
        out_shape=jax.ShapeDtypeStruct((n_slots, d), F32),
        scratch_shapes=[pltpu.SemaphoreType.DMA(())],
        input_output_aliases={4: 0},
        compiler_params=_cparams("arbitrary"),
        name="moe_dispatch",
    )(eid, rank, pad_starts, h, jnp.zeros((n_slots, d), F32))


def _expert_kernel(be_ref, nused_ref, xs_ref, wg_ref, wu_ref, wd_ref, y_ref):
    del be_ref

    @pl.when(pl.program_id(0) < nused_ref[0])
    def _():
        x = xs_ref[...].astype(BF16)
        gt = _dot(x, wg_ref[...])
        up = _dot(x, wu_ref[...])
        mid = (gt * jax.nn.sigmoid(gt) * up).astype(BF16)
        y_ref[...] = _dot(mid, wd_ref[...])

    @pl.when(pl.program_id(0) >= nused_ref[0])
    def _():
        y_ref[...] = jnp.zeros_like(y_ref)


def _experts(xs, block_expert, n_used, w_gate, w_up, w_down):
    n_slots, d = xs.shape
    de = w_gate.shape[-1]
    nb = n_slots // MOE_ROWS
    grid_spec = pltpu.PrefetchScalarGridSpec(
        num_scalar_prefetch=2,
        grid=(nb,),
        in_specs=[pl.BlockSpec((MOE_ROWS, d), lambda b, be, nu: (b, 0)),
                  pl.BlockSpec((None, d, de), lambda b, be, nu: (be[b], 0, 0)),
                  pl.BlockSpec((None, d, de), lambda b, be, nu: (be[b], 0, 0)),
                  pl.BlockSpec((None, de, d), lambda b, be, nu: (be[b], 0, 0))],
        out_specs=pl.BlockSpec((MOE_ROWS, d), lambda b, be, nu: (b, 0)),
    )
    return pl.pallas_call(
        _expert_kernel,
        grid_spec=grid_spec,
        out_shape=jax.ShapeDtypeStruct((n_slots, d), F32),
        compiler_params=_cparams("arbitrary"),
        name="moe_experts",
    )(block_expert, n_used, xs, w_gate, w_up, w_down)


def _combine_kernel(eid_ref, rank_ref, start_ref, gate_ref, x_ref, g2_ref, fin_ref, yb_hbm, o_ref, buf_ref, sem,
                    *, tm, final_norm):
    def row_copy(t, kk):
        slot = start_ref[eid_ref[kk, t]] + rank_ref[kk, t]
        return pltpu.make_async_copy(yb_hbm.at[pl.ds(slot, 1)], buf_ref.at[kk, pl.ds(t, 1)], sem)

    def issue(t, c):
        row_copy(t, 0).start()
        row_copy(t, 1).start()
        return c

    def drain(t, c):
        row_copy(t, 0).wait()
        row_copy(t, 1).wait()
        return c

    lax.fori_loop(0, tm, issue, 0)
    eye = lax.broadcasted_iota(jnp.int32, (tm, tm), 0) == lax.broadcasted_iota(jnp.int32, (tm, tm), 1)
    gates = gate_ref[...]
    w0 = jnp.sum(jnp.where(eye, gates[0:1, :], 0.0), axis=1, keepdims=True)
    w1 = jnp.sum(jnp.where(eye, gates[1:2, :], 0.0), axis=1, keepdims=True)
    lax.fori_loop(0, tm, drain, 0)
    y = w0 * buf_ref[0] + w1 * buf_ref[1]
    out = x_ref[...] + g2_ref[...] * y
    if final_norm:
        out = out * lax.rsqrt(jnp.mean(out * out, axis=-1, keepdims=True) + NORM_EPS) * fin_ref[...]
    o_ref[...] = out


def _combine(yb, eid, rank, gate, pad_starts, x, mod_l, final_g, final_norm, seq):
    n, d = x.shape
    tm = min(256, seq)
    tps = seq // tm
    smem_tile = pl.BlockSpec((2, tm), lambda i: (0, i), memory_space=pltpu.SMEM)
    return pl.pallas_call(
        functools.partial(_combine_kernel, tm=tm, final_norm=final_norm),
        grid=(n // tm,),
        in_specs=[smem_tile, smem_tile,
                  pl.BlockSpec(memory_space=pltpu.SMEM),
                  pl.BlockSpec((2, tm), lambda i: (0, i)),
                  pl.BlockSpec((tm, d), lambda i: (i, 0)),
                  _mod_spec(5, tps),
                  pl.BlockSpec((1, d), lambda i: (0, 0)),
                  pl.BlockSpec(memory_space=pl.ANY)],
        out_specs=pl.BlockSpec((tm, d), lambda i: (i, 0)),
        out_shape=jax.ShapeDtypeStruct((n, d), F32),
        scratch_shapes=[pltpu.VMEM((2, tm, d), F32), pltpu.SemaphoreType.DMA(())],
        compiler_params=_cparams("arbitrary"),
        name="moe_combine",
    )(eid, rank, pad_starts, gate, x, mod_l, final_g.reshape(1, d), yb)


def _grouped_moe_residual(x, g, mod_l, router_w, router_bias, w_gate, w_up, w_down, final_g, final_norm, seq):
    n, d = x.shape
    h, eid, gate, rank, counts = _route(x, g, mod_l, router_w, router_bias, seq)
    counts = counts[:, 0]
    padded = (counts + MOE_ROWS - 1) // MOE_ROWS * MOE_ROWS
    pad_ends = jnp.cumsum(padded)
    pad_starts = (pad_ends - padded).astype(jnp.int32)
    n_slots = 2 * n + N_EXPERTS * MOE_ROWS
    nb = n_slots // MOE_ROWS
    block_start = jnp.arange(nb, dtype=jnp.int32) * MOE_ROWS
    block_expert = jnp.minimum(jnp.searchsorted(pad_ends, block_start, side='right'), N_EXPERTS - 1).astype(jnp.int32)
    n_used = (pad_ends[-1:] // MOE_ROWS).astype(jnp.int32)
    xs = _dispatch(h, eid, rank, pad_starts, n_slots)
    yb = _experts(xs, block_expert, n_used, w_gate, w_up, w_down)
    return _combine(yb, eid, rank, gate, pad_starts, x, mod_l, final_g, final_norm, seq)


def kernel(x, c, w_mod, b_mod, norm1_g, norm2_g, final_g, attn_w_in, attn_w_out, diff_lambda, diff_subln_g, rwkv_mix, rwkv_w_rkv, rwkv_w_out, rwkv_w0, rwkv_w_lora_a, rwkv_w_lora_b, rwkv_a0, rwkv_a_lora_a, rwkv_a_lora_b, rwkv_g_lora_a, rwkv_g_lora_b, rwkv_v0, rwkv_v_lora_a, rwkv_v_lora_b, rwkv_k_k, rwkv_k_a, rwkv_r_k, rwkv_lnx_g, rwkv_lnx_b, router_w, router_bias, moe_w_gate, moe_w_up, moe_w_down):
    bsz, seq, d = x.shape
    depth = w_mod.shape[0]
    n = bsz * seq
    mod = _modulation(c.astype(F32), w_mod, b_mod)
    xf = x.astype(F32).reshape(n, d)
    v_first = None
    for l in range(depth):
        mod_l = mod[l]
        if l % 2 == 0:
            e = l // 2
            lambda_init = 0.8 - 0.6 * math.exp(-0.3 * l)
            proj = _norm_matmul(xf, norm1_g[l], mod_l, attn_w_in[e].astype(BF16), seq)
            o_a = _sb_attention(proj, bsz, seq)
            o_b = _diff_attention(proj, diff_lambda[e], diff_subln_g[e], lambda_init, bsz, seq)
            xf = _out_proj([o_a, o_b], attn_w_out[e].astype(BF16), xf, mod_l, 2, seq)
        else:
            o = l // 2
            v_res = None if o == 0 else (rwkv_v0[o - 1], rwkv_v_lora_a[o - 1], rwkv_v_lora_b[o - 1])
            outs = _rwkv_proj(xf, norm1_g[l], mod_l, rwkv_mix[o], rwkv_w_rkv[o], rwkv_w0[o], rwkv_w_lora_a[o],
                              rwkv_w_lora_b[o], rwkv_a0[o], rwkv_a_lora_a[o], rwkv_a_lora_b[o], rwkv_g_lora_a[o],
                              rwkv_g_lora_b[o], v_res, seq)
            rkv, logw, a, g = outs[:4]
            vg = outs[4] if v_res is not None else None
            if v_res is None:
                v_first = rkv
            yg = _wkv(rkv, logw, a, g, vg, v_first, rwkv_k_k[o], rwkv_k_a[o], rwkv_r_k[o].reshape(d),
                      rwkv_lnx_g[o], rwkv_lnx_b[o], bsz, seq)
            xf = _out_proj([yg], rwkv_w_out[o].astype(BF16), xf, mod_l, 2, seq)
        xf = _grouped_moe_residual(xf, norm2_g[l], mod_l, router_w, router_bias, moe_w_gate[l].astype(BF16),
                                   moe_w_up[l].astype(BF16), moe_w_down[l].astype(BF16), final_g,
                                   l == depth - 1, seq)
    return xf.reshape(bsz, seq, d)
```

```python
import functools
import math

import jax
import jax.numpy as jnp
from jax import lax
from jax.experimental import pallas as pl
from jax.experimental.pallas import tpu as pltpu

F32 = jnp.float32
BF16 = jnp.bfloat16
HIGHEST = lax.Precision.HIGHEST

D_MODEL = 2048
SB_HEADS = 8
SB_HEAD_DIM = 128
DIFF_HEADS = 8
DIFF_QK_DIM = 64
DIFF_V_DIM = 128
SB_WIDTH = SB_HEADS * SB_HEAD_DIM
DIFF_QK_WIDTH = DIFF_HEADS * 2 * DIFF_QK_DIM
DIFF_V_WIDTH = DIFF_HEADS * DIFF_V_DIM
PROJ_WIDTH = 3 * SB_WIDTH + 2 * DIFF_QK_WIDTH + DIFF_V_WIDTH
RWKV_HEAD_DIM = 64
RWKV_GN_EPS = 64e-5
N_EXPERTS = 16
N_GROUPS = 4
EXPERTS_PER_GROUP = 4
D_EXPERT = D_MODEL // 2
NORM_EPS = 1e-6

LANES = 128
WKV_CHUNK = 64
WKV_LANES = 256
WKV_HEADS = WKV_LANES // RWKV_HEAD_DIM
MOE_ROWS = 256
NEG_BIG = -1e30
VMEM_LIMIT = 56 * 1024 * 1024


def _cparams(*sem):
    return pltpu.CompilerParams(dimension_semantics=sem, vmem_limit_bytes=VMEM_LIMIT)


def _dot(a, b):
    return jnp.dot(a, b, preferred_element_type=F32)


def _dot_nt(a, b):
    return lax.dot_general(a, b, (((1,), (1,)), ((), ())), preferred_element_type=F32)


def _dot_tn(a, b):
    return lax.dot_general(a, b, (((0,), (0,)), ((), ())), preferred_element_type=F32)


def _softplus(z):
    return jnp.maximum(z, 0.0) + jnp.log1p(jnp.exp(-jnp.abs(z)))


def _modnorm(x, g, scale, shift):
    ms = jnp.mean(x * x, axis=-1, keepdims=True)
    return x * lax.rsqrt(ms + NORM_EPS) * g * (1.0 + scale) + shift


def _mod_spec(which, rows_per_batch_tile):
    return pl.BlockSpec((None, None, 1, D_MODEL), lambda i, *_: (i // rows_per_batch_tile, which, 0, 0))


def _mod_kernel(c_ref, w_ref, b_ref, o_ref):
    c = c_ref[...]
    cond = c * jax.nn.sigmoid(c)
    o_ref[...] = jnp.dot(cond, w_ref[...], preferred_element_type=F32, precision=HIGHEST) + b_ref[...]


def _modulation(c, w_mod, b_mod):
    depth, d, n6 = w_mod.shape
    bsz = c.shape[0]
    tn = 1536
    out = pl.pallas_call(
        _mod_kernel,
        grid=(depth, n6 // tn),
        in_specs=[pl.BlockSpec((bsz, d), lambda l, j: (0, 0)),
                  pl.BlockSpec((None, d, tn), lambda l, j: (l, 0, j)),
                  pl.BlockSpec((None, 1, tn), lambda l, j: (l, 0, j))],
        out_specs=pl.BlockSpec((None, bsz, tn), lambda l, j: (l, 0, j)),
        out_shape=jax.ShapeDtypeStruct((depth, bsz, n6), F32),
        compiler_params=_cparams("arbitrary", "arbitrary"),
        name="adaln_modulation",
    )(c, w_mod, b_mod.reshape(depth, 1, n6))
    return out.reshape(depth, bsz, 6, 1, d)


def _normmm_kernel(x_ref, g_ref, sh_ref, sc_ref, w_ref, o_ref, h_ref):
    @pl.when(pl.program_id(1) == 0)
    def _():
        h_ref[...] = _modnorm(x_ref[...], g_ref[...], sc_ref[...], sh_ref[...]).astype(BF16)

    o_ref[...] = _dot(h_ref[...], w_ref[...]).astype(o_ref.dtype)


def _norm_matmul(x, g, mod_l, w, seq):
    n, d = x.shape
    nout = w.shape[1]
    tm = min(1024, seq)
    tn = 512
    tps = seq // tm
    return pl.pallas_call(
        _normmm_kernel,
        grid=(n // tm, nout // tn),
        in_specs=[pl.BlockSpec((tm, d), lambda i, j: (i, 0)),
                  pl.BlockSpec((1, d), lambda i, j: (0, 0)),
                  _mod_spec(0, tps), _mod_spec(1, tps),
                  pl.BlockSpec((d, tn), lambda i, j: (0, j))],
        out_specs=pl.BlockSpec((tm, tn), lambda i, j: (i, j)),
        out_shape=jax.ShapeDtypeStruct((n, nout), BF16),
        scratch_shapes=[pltpu.VMEM((tm, d), BF16)],
        compiler_params=_cparams("arbitrary", "arbitrary"),
        name="attn_in_proj",
    )(x, g.reshape(1, d), mod_l, mod_l, w)


def _outproj_kernel(*refs, n_lhs):
    lhs = refs[:n_lhs]
    ws = refs[n_lhs:2 * n_lhs]
    x_ref, gate_ref, o_ref = refs[2 * n_lhs:]
    acc = _dot(lhs[0][...], ws[0][...])
    for a_ref, w_ref in zip(lhs[1:], ws[1:]):
        acc = acc + _dot(a_ref[...], w_ref[...])
    o_ref[...] = x_ref[...] + gate_ref[...] * acc


def _out_proj(lhs_list, w, x, mod_l, gate_idx, seq):
    n, d = x.shape
    tm = min(1024, seq)
    tn = 512
    tps = seq // tm
    n_lhs = len(lhs_list)
    kp = w.shape[0] // n_lhs
    in_specs = [pl.BlockSpec((tm, kp), lambda i, j: (i, 0)) for _ in lhs_list]
    in_specs += [pl.BlockSpec((kp, tn), functools.partial(lambda i, j, p: (p, j), p=p)) for p in range(n_lhs)]
    in_specs += [pl.BlockSpec((tm, tn), lambda i, j: (i, j)),
                 pl.BlockSpec((None, None, 1, tn), lambda i, j: (i // tps, gate_idx, 0, j))]
    return pl.pallas_call(
        functools.partial(_outproj_kernel, n_lhs=n_lhs),
        grid=(n // tm, d // tn),
        in_specs=in_specs,
        out_specs=pl.BlockSpec((tm, tn), lambda i, j: (i, j)),
        out_shape=jax.ShapeDtypeStruct((n, d), F32),
        compiler_params=_cparams("arbitrary", "arbitrary"),
        name="out_proj_residual",
    )(*lhs_list, *([w] * n_lhs), x, mod_l)


def _sb_kernel(q_ref, k_ref, v_ref, o_ref, *, tq, tk, scale):
    qi = pl.program_id(2)
    q = q_ref[...]
    jj = lax.broadcasted_iota(jnp.int32, (tk, tk), 0)
    ss = lax.broadcasted_iota(jnp.int32, (tk, tk), 1)
    later = jnp.where(jj > ss, 1.0, 0.0).astype(BF16)
    rel0 = (lax.broadcasted_iota(jnp.int32, (tq, tk), 0) - lax.broadcasted_iota(jnp.int32, (tq, tk), 1))

    def block(kb, carry, acc, masked):
        start = pl.multiple_of(kb * tk, tk)
        k = k_ref[pl.ds(start, tk), :]
        v = v_ref[pl.ds(start, tk), :]
        z = _dot_nt(q, k) * scale
        sp = _softplus(z)
        log_keep = -sp
        if masked:
            strict = (rel0 + (qi * tq - kb * tk)) > 0
            log_keep = jnp.where(strict, log_keep, 0.0)
        hi = log_keep.astype(BF16)
        lo = (log_keep - hi.astype(F32)).astype(BF16)
        cs = _dot(jnp.concatenate([hi, lo], axis=0), later)
        log_between = cs[:tq] + cs[tq:] + carry
        w = jnp.exp(z - sp + log_between)
        if masked:
            w = jnp.where(strict, w, 0.0)
        acc = acc + _dot(w.astype(BF16), v)
        carry = carry + jnp.sum(log_keep, axis=1, keepdims=True)
        return carry, acc

    carry = jnp.zeros((tq, 1), F32)
    acc = jnp.zeros((tq, SB_HEAD_DIM), F32)
    n_full = (qi * tq) // tk
    last = ((qi + 1) * tq - 1) // tk
    for m in range(max(tq // tk, 1)):
        carry, acc = block(last - m, carry, acc, True)

    def body(i, c):
        return block(n_full - 1 - i, c[0], c[1], False)

    carry, acc = lax.fori_loop(0, n_full, body, (carry, acc))
    o_ref[...] = acc.astype(o_ref.dtype)


def _sb_attention(proj, bsz, seq):
    n = proj.shape[0]
    tq = min(256, seq)
    tk = min(128, seq)
    nq = seq // tq
    kern = functools.partial(_sb_kernel, tq=tq, tk=tk, scale=SB_HEAD_DIM ** -0.5)
    return pl.pallas_call(
        kern,
        grid=(bsz, SB_HEADS, nq),
        in_specs=[pl.BlockSpec((tq, LANES), lambda b, h, i: (b * nq + i, h)),
                  pl.BlockSpec((seq, LANES), lambda b, h, i: (b, SB_HEADS + h)),
                  pl.BlockSpec((seq, LANES), lambda b, h, i: (b, 2 * SB_HEADS + h))],
        out_specs=pl.BlockSpec((tq, LANES), lambda b, h, i: (b * nq + i, h)),
        out_shape=jax.ShapeDtypeStruct((n, SB_WIDTH), BF16),
        compiler_params=_cparams("arbitrary", "arbitrary", "arbitrary"),
        name="stick_breaking_attention",
    )(proj, proj, proj)


def _diff_kernel(q_ref, k_ref, v_ref, slope_ref, lam_ref, g_ref, o_ref, *, tq, tk, scale, lambda_init):
    qi = pl.program_id(2)
    q = q_ref[...]
    lane = lax.broadcasted_iota(jnp.int32, (tq, LANES), 1)
    zero = jnp.zeros_like(q)
    q2 = jnp.concatenate([jnp.where(lane < DIFF_QK_DIM, q, zero), jnp.where(lane >= DIFF_QK_DIM, q, zero)], axis=0)
    row = lax.broadcasted_iota(jnp.int32, (2 * tq, tk), 0)
    rel0 = jnp.where(row >= tq, row - tq, row) - lax.broadcasted_iota(jnp.int32, (2 * tq, tk), 1)
    slope = slope_ref[...]

    def block(kb, m, l, acc, masked):
        start = pl.multiple_of(kb * tk, tk)
        k = k_ref[pl.ds(start, tk), :]
        v = v_ref[pl.ds(start, tk), :]
        rel = rel0 + (qi * tq - kb * tk)
        s = _dot_nt(q2, k) * scale - slope * rel.astype(F32)
        if masked:
            s = jnp.where(rel >= 0, s, NEG_BIG)
        m_new = jnp.maximum(m, jnp.max(s, axis=1, keepdims=True))
        alpha = jnp.exp(m - m_new)
        p = jnp.exp(s - m_new)
        l = alpha * l + jnp.sum(p, axis=1, keepdims=True)
        acc = alpha * acc + _dot(p.astype(BF16), v)
        return m_new, l, acc

    m = jnp.full((2 * tq, 1), NEG_BIG, F32)
    l = jnp.zeros((2 * tq, 1), F32)
    acc = jnp.zeros((2 * tq, DIFF_V_DIM), F32)
    n_full = (qi * tq) // tk
    last = ((qi + 1) * tq - 1) // tk
    m, l, acc = lax.fori_loop(0, n_full, lambda kb, c: block(kb, c[0], c[1], c[2], False), (m, l, acc))
    n_masked = max(tq // tk, 1)
    for j in range(n_masked):
        m, l, acc = block(last - (n_masked - 1 - j), m, l, acc, True)

    lam4 = lam_ref[...]
    lam = (jnp.exp(jnp.sum(lam4[0:1] * lam4[1:2], axis=1, keepdims=True))
           - jnp.exp(jnp.sum(lam4[2:3] * lam4[3:4], axis=1, keepdims=True)) + lambda_init)
    o = acc[:tq] / l[:tq] - lam * (acc[tq:] / l[tq:])
    o = o * lax.rsqrt(jnp.mean(o * o, axis=-1, keepdims=True) + NORM_EPS) * g_ref[...] * (1.0 - lambda_init)
    o_ref[...] = o.astype(o_ref.dtype)


def _diff_attention(proj, diff_lambda, subln_g, lambda_init, bsz, seq):
    n = proj.shape[0]
    tq = min(128, seq)
    tk = min(256, seq)
    nq = seq // tq
    qoff = 3 * SB_HEADS
    slopes = 2.0 ** (-8.0 * (jnp.arange(DIFF_HEADS, dtype=F32) + 1.0) / DIFF_HEADS)
    slopes = jnp.broadcast_to(slopes[:, None, None], (DIFF_HEADS, 1, tk))
    kern = functools.partial(_diff_kernel, tq=tq, tk=tk, scale=DIFF_QK_DIM ** -0.5, lambda_init=lambda_init)
    return pl.pallas_call(
        kern,
        grid=(bsz, DIFF_HEADS, nq),
        in_specs=[pl.BlockSpec((tq, LANES), lambda b, h, i: (b * nq + i, qoff + h)),
                  pl.BlockSpec((seq, LANES), lambda b, h, i: (b, qoff + DIFF_HEADS + h)),
                  pl.BlockSpec((seq, LANES), lambda b, h, i: (b, qoff + 2 * DIFF_HEADS + h)),
                  pl.BlockSpec((None, 1, tk), lambda b, h, i: (h, 0, 0)),
                  pl.BlockSpec((4, DIFF_QK_DIM), lambda b, h, i: (0, 0)),
                  pl.BlockSpec((1, DIFF_V_DIM), lambda b, h, i: (0, 0))],
        out_specs=pl.BlockSpec((tq, LANES), lambda b, h, i: (b * nq + i, h)),
        out_shape=jax.ShapeDtypeStruct((n, DIFF_V_WIDTH), BF16),
        compiler_params=_cparams("arbitrary", "arbitrary", "arbitrary"),
        name="differential_attention",
    )(proj, proj, proj, slopes, diff_lambda, subln_g.reshape(1, DIFF_V_DIM))


RKV_TILES = 12
LORA_TILES = 4
RWKV_TN = 512


def _rwkv_proj_kernel(*refs, tm, tiles_per_seq, has_vres):
    (x_ref, xp_ref, g_ref, sh_ref, sc_ref, mix_ref, w_ref, wla_ref, ala_ref, gla_ref,
     wlb_ref, alb_ref, glb_ref, w0_ref, a0_ref) = refs[:15]
    pos = 15
    if has_vres:
        vla_ref, vlb_ref, v0_ref = refs[pos:pos + 3]
        pos += 3
    rkv_ref, logw_ref, a_out_ref, g_out_ref = refs[pos:pos + 4]
    pos += 4
    if has_vres:
        vg_out_ref = refs[pos]
        pos += 1
    lerp_ref, l1w_ref, l1a_ref, l1g_ref = refs[pos:pos + 4]
    pos += 4
    if has_vres:
        l1v_ref = refs[pos]

    i = pl.program_id(0)
    j = pl.program_id(1)

    @pl.when(j == 0)
    def _():
        g, sc, sh = g_ref[...], sc_ref[...], sh_ref[...]
        h = _modnorm(x_ref[...], g, sc, sh)
        hp = _modnorm(xp_ref[...], g, sc, sh)[7:8, :]
        hp = jnp.where(i % tiles_per_seq == 0, jnp.zeros_like(hp), hp)
        first_row = lax.broadcasted_iota(jnp.int32, h.shape, 0) == 0
        dx = jnp.where(first_row, hp, pltpu.roll(h, 1, 0)) - h
        for m in range(6):
            lerp_ref[m] = (h + dx * mix_ref[m:m + 1, :]).astype(BF16)

    @pl.when(j < RKV_TILES)
    def _():
        sel = j // (RKV_TILES // 3)
        idx = jnp.where(sel == 0, 0, sel + 1)
        rkv_ref[...] = _dot(lerp_ref[idx], w_ref[...]).astype(rkv_ref.dtype)

    @pl.when(j == RKV_TILES)
    def _():
        l1w_ref[...] = jnp.tanh(_dot(lerp_ref[1], wla_ref[...])).astype(BF16)
        l1a_ref[...] = _dot(lerp_ref[4], ala_ref[...]).astype(BF16)
        l1g_ref[...] = jax.nn.sigmoid(_dot(lerp_ref[5], gla_ref[...])).astype(BF16)
        if has_vres:
            l1v_ref[...] = _dot(lerp_ref[3], vla_ref[...]).astype(BF16)

    @pl.when(j >= RKV_TILES)
    def _():
        pre = w0_ref[...] + _dot(l1w_ref[...], wlb_ref[...])
        logw_ref[...] = -jnp.exp(-_softplus(-pre) - 0.5)
        a_out_ref[...] = jax.nn.sigmoid(a0_ref[...] + _dot(l1a_ref[...], alb_ref[...])).astype(BF16)
        g_out_ref[...] = _dot(l1g_ref[...], glb_ref[...]).astype(BF16)
        if has_vres:
            vg_out_ref[...] = jax.nn.sigmoid(v0_ref[...] + _dot(l1v_ref[...], vlb_ref[...])).astype(BF16)


def _pad_cols(w, to):
    return jnp.pad(w, ((0, 0), (0, to - w.shape[1])))


def _pad_rows(w, to):
    return jnp.pad(w, ((0, to - w.shape[0]), (0, 0)))


def _rwkv_proj(x, g, mod_l, mix, w_rkv, w0, w_la, w_lb, a0, a_la, a_lb, g_la, g_lb, v_res, seq):
    n, d = x.shape
    tm = min(512, seq)
    tn = RWKV_TN
    tps = seq // tm
    has_vres = v_res is not None
    wcat = jnp.concatenate([w_rkv[0], w_rkv[1], w_rkv[2]], axis=1).astype(BF16)
    lw = LANES
    gl = g_la.shape[1]

    def rkv_col(i, j):
        return (0, jnp.minimum(j, RKV_TILES - 1))

    def lora_col(i, j):
        return (0, jnp.clip(j - RKV_TILES, 0, LORA_TILES - 1))

    def full(shape):
        return pl.BlockSpec(shape, lambda i, j: (0, 0))

    in_specs = [pl.BlockSpec((tm, d), lambda i, j: (i, 0)),
                pl.BlockSpec((8, d), lambda i, j: (jnp.maximum(i * (tm // 8) - 1, 0), 0)),
                full((1, d)), _mod_spec(0, tps), _mod_spec(1, tps), full((6, d)),
                pl.BlockSpec((d, tn), rkv_col),
                full((d, lw)), full((d, lw)), full((d, gl)),
                pl.BlockSpec((lw, tn), lora_col), pl.BlockSpec((lw, tn), lora_col), pl.BlockSpec((gl, tn), lora_col),
                pl.BlockSpec((1, tn), lora_col), pl.BlockSpec((1, tn), lora_col)]
    args = [x, x, g.reshape(1, d), mod_l, mod_l, mix, wcat,
            _pad_cols(w_la, lw).astype(BF16), _pad_cols(a_la, lw).astype(BF16), g_la.astype(BF16),
            _pad_rows(w_lb, lw).astype(BF16), _pad_rows(a_lb, lw).astype(BF16), g_lb.astype(BF16),
            w0.reshape(1, d), a0.reshape(1, d)]
    out_tile = lambda i, j: (i, jnp.clip(j - RKV_TILES, 0, LORA_TILES - 1))
    out_specs = [pl.BlockSpec((tm, tn), lambda i, j: (i, jnp.minimum(j, RKV_TILES - 1))),
                 pl.BlockSpec((tm, tn), out_tile), pl.BlockSpec((tm, tn), out_tile), pl.BlockSpec((tm, tn), out_tile)]
    out_shape = [jax.ShapeDtypeStruct((n, 3 * d), BF16), jax.ShapeDtypeStruct((n, d), F32),
                 jax.ShapeDtypeStruct((n, d), BF16), jax.ShapeDtypeStruct((n, d), BF16)]
    scratch = [pltpu.VMEM((6, tm, d), BF16), pltpu.VMEM((tm, lw), BF16), pltpu.VMEM((tm, lw), BF16),
               pltpu.VMEM((tm, gl), BF16)]
    if has_vres:
        v0, v_la, v_lb = v_res
        in_specs += [full((d, lw)), pl.BlockSpec((lw, tn), lora_col), pl.BlockSpec((1, tn), lora_col)]
        args += [_pad_cols(v_la, lw).astype(BF16), _pad_rows(v_lb, lw).astype(BF16), v0.reshape(1, d)]
        out_specs.append(pl.BlockSpec((tm, tn), out_tile))
        out_shape.append(jax.ShapeDtypeStruct((n, d), BF16))
        scratch.append(pltpu.VMEM((tm, lw), BF16))
    return pl.pallas_call(
        functools.partial(_rwkv_proj_kernel, tm=tm, tiles_per_seq=tps, has_vres=has_vres),
        grid=(n // tm, RKV_TILES + LORA_TILES),
        in_specs=in_specs, out_specs=out_specs, out_shape=out_shape, scratch_shapes=scratch,
        compiler_params=_cparams("arbitrary", "arbitrary"),
        name="rwkv_projections",
    )(*args)


def _split3(x):
    hi = x.astype(BF16)
    r1 = x - hi.astype(F32)
    mid = r1.astype(BF16)
    lo = (r1 - mid.astype(F32)).astype(BF16)
    return hi, mid, lo


def _unit_lower_inverse(a, row, col):
    def same_block(nb):
        return (row // nb) == (col // nb)

    eye = jnp.where(row == col, 1.0, 0.0)
    b16 = same_block(16)
    ad = jnp.where(b16, a, 0.0)
    adb = ad.astype(BF16)
    a2 = _dot(adb, adb)
    a2b = a2.astype(BF16)
    a4 = _dot(a2b, a2b)
    a4b = a4.astype(BF16)
    a8 = _dot(a4b, a4b)
    t = eye + ad + a2 + _dot(adb, a2b)
    t = t + _dot(t.astype(BF16), a4b)
    t = t + _dot(t.astype(BF16), a8.astype(BF16))
    prev = b16
    for nb in (32, 64):
        cur = same_block(nb)
        off = jnp.where(jnp.logical_and(cur, jnp.logical_not(prev)), a, 0.0)
        tb = t.astype(BF16)
        t = t + _dot(_dot(tb, off.astype(BF16)).astype(BF16), tb)
        prev = cur
    return t


def _wkv_kernel(*refs, n_chunks, has_vres):
    (r_ref, k_ref, v_ref, lw_ref, a_ref, g_ref) = refs[:6]
    pos = 6
    if has_vres:
        vg_ref, vf_ref = refs[pos:pos + 2]
        pos += 2
    kk_ref, ka_ref, rk_ref, lng_ref, lnb_ref = refs[pos:pos + 5]
    pos += 5
    y_ref = refs[pos]
    s_ref = refs[pos + 1]

    L, W = WKV_CHUNK, WKV_LANES
    R = WKV_HEADS * L

    @pl.when(pl.program_id(2) == 0)
    def _():
        s_ref[...] = jnp.zeros_like(s_ref)

    lane = lax.broadcasted_iota(jnp.int32, (L, W), 1)
    head_masks = [(lane // RWKV_HEAD_DIM) == h for h in range(WKV_HEADS)]
    row = lax.broadcasted_iota(jnp.int32, (R, R), 0)
    col = lax.broadcasted_iota(jnp.int32, (R, R), 1)
    strict = row > col
    incl = row >= col
    tt = lax.broadcasted_iota(jnp.int32, (L, L), 0)
    tj = lax.broadcasted_iota(jnp.int32, (L, L), 1)
    upto = jnp.where(tj <= tt, 1.0, 0.0).astype(BF16)

    def stack(z):
        return jnp.concatenate([jnp.where(mk, z, 0.0) for mk in head_masks], axis=0)

    def head_sum(z):
        out = jnp.zeros_like(z)
        for mk in head_masks:
            out = jnp.where(mk, jnp.sum(jnp.where(mk, z, 0.0), axis=1, keepdims=True), out)
        return out

    k_k, k_a, r_k = kk_ref[...], ka_ref[...], rk_ref[...]
    ln_g, ln_b = lng_ref[...], lnb_ref[...]

    def chunk(c, carry):
        rows = pl.ds(pl.multiple_of(c * L, L), L)
        r = r_ref[rows, :].astype(F32)
        k = k_ref[rows, :].astype(F32)
        v = v_ref[rows, :].astype(F32)
        a = a_ref[rows, :].astype(F32)
        lw = lw_ref[rows, :]
        if has_vres:
            v = v + (vf_ref[rows, :].astype(F32) - v) * vg_ref[rows, :].astype(F32)
        kk = k * k_k
        kk = kk / jnp.maximum(jnp.sqrt(head_sum(kk * kk)), 1e-12)
        km = k * (1.0 + (a - 1.0) * k_a)
        bv = kk * a

        hi, mid, lo = _split3(lw)
        cs = _dot(upto, jnp.concatenate([hi, mid, lo], axis=1))
        cl = cs[:, :W] + cs[:, W:2 * W] + cs[:, 2 * W:]
        p_in = jnp.exp(cl)
        p_inv = jnp.exp(-cl)
        p_last = p_in[L - 1:L, :]
        at = -kk * jnp.exp(cl - lw)
        rt = r * p_in
        bt = bv * p_inv
        kt = km * p_inv

        s0 = s_ref[...]
        ar = jnp.concatenate([stack(at), stack(rt)], axis=0).astype(BF16)
        bk = jnp.concatenate([stack(bt), stack(kt)], axis=0).astype(BF16)
        gm = _dot_nt(ar, bk)
        a_ab = jnp.where(strict, gm[:R, :R], 0.0)
        a_ak = jnp.where(strict, gm[:R, R:], 0.0)
        a_rb = jnp.where(incl, gm[R:, :R], 0.0)
        a_rk = jnp.where(incl, gm[R:, R:], 0.0)
        ss = _dot_nt(ar, s0.astype(BF16))
        vx = stack(v)
        vxb = vx.astype(BF16)
        x = ss[:R] + _dot(a_ak.astype(BF16), vxb)
        u = _dot(_unit_lower_inverse(a_ab, row, col).astype(BF16), x.astype(BF16))
        uv = jnp.concatenate([u.astype(BF16), vxb], axis=0)
        yx = ss[R:] + _dot(jnp.concatenate([a_rb, a_rk], axis=1).astype(BF16), uv)
        y = yx[0:L]
        for h in range(1, WKV_HEADS):
            y = y + yx[h * L:(h + 1) * L]
        bkl = jnp.concatenate([stack(bt * p_last), stack(kt * p_last)], axis=0).astype(BF16)
        s_ref[...] = s0 * p_last + _dot_tn(uv, bkl)

        inv_n = 1.0 / RWKV_HEAD_DIM
        mu = head_sum(y) * inv_n
        dlt = y - mu
        var = head_sum(dlt * dlt) * inv_n
        yn = dlt * lax.rsqrt(var + RWKV_GN_EPS) * ln_g + ln_b
        yn = yn + head_sum(r * km * r_k) * v
        y_ref[rows, :] = (yn * g_ref[rows, :].astype(F32)).astype(y_ref.dtype)
        return carry

    lax.fori_loop(0, n_chunks, chunk, 0)


def _wkv(rkv, logw, a, g, vg, v_first, k_k, k_a, r_k, lnx_g, lnx_b, bsz, seq):
    n = logw.shape[0]
    d = D_MODEL
    W = WKV_LANES
    rows = min(256, seq)
    n_chunks = rows // WKV_CHUNK
    nt = seq // rows
    nw = d // W
    has_vres = vg is not None

    def tile(off):
        return pl.BlockSpec((rows, W), functools.partial(lambda b, w, t, off: (b * nt + t, off + w), off=off))

    vec = pl.BlockSpec((1, W), lambda b, w, t: (0, w))
    in_specs = [tile(0), tile(nw), tile(2 * nw), tile(0), tile(0), tile(0)]
    args = [rkv, rkv, rkv, logw, a, g]
    if has_vres:
        in_specs += [tile(0), tile(2 * nw)]
        args += [vg, v_first]
    in_specs += [vec] * 5
    args += [z.reshape(1, d) for z in (k_k, k_a, r_k, lnx_g, lnx_b)]
    return pl.pallas_call(
        functools.partial(_wkv_kernel, n_chunks=n_chunks, has_vres=has_vres),
        grid=(bsz, nw, nt),
        in_specs=in_specs,
        out_specs=tile(0),
        out_shape=jax.ShapeDtypeStruct((n, d), BF16),
        scratch_shapes=[pltpu.VMEM((W, W), F32)],
        compiler_params=_cparams("arbitrary", "arbitrary", "arbitrary"),
        name="rwkv7_chunked_state",
    )(*args)


def _route_kernel(x_ref, g_ref, sh_ref, sc_ref, rw_ref, rb_ref, h_ref, eid_ref, gate_ref, rank_ref, cnt_ref,
                  base_ref, before_ref, *, tm):
    step = pl.program_id(0)

    @pl.when(step == 0)
    def _():
        base_ref[...] = jnp.zeros_like(base_ref)
        n0 = lax.broadcasted_iota(jnp.int32, (tm, tm), 0)
        n1 = lax.broadcasted_iota(jnp.int32, (tm, tm), 1)
        before_ref[...] = jnp.where(n0 < n1, 1.0, 0.0).astype(BF16)

    h = _modnorm(x_ref[...], g_ref[...], sc_ref[...], sh_ref[...])
    h_ref[...] = h
    logits = lax.dot_general(rw_ref[...], h, (((1,), (1,)), ((), ())), preferred_element_type=F32,
                             precision=HIGHEST)
    scores = jax.nn.sigmoid(logits)
    sel = scores + rb_ref[...]

    def rows_of(z, grp):
        return [z[grp * EXPERTS_PER_GROUP + j:grp * EXPERTS_PER_GROUP + j + 1, :] for j in range(EXPERTS_PER_GROUP)]

    group_score = []
    for grp in range(N_GROUPS):
        a, b, c, d = rows_of(sel, grp)
        hi1, lo1 = jnp.maximum(a, b), jnp.minimum(a, b)
        hi2, lo2 = jnp.maximum(c, d), jnp.minimum(c, d)
        group_score.append(jnp.maximum(hi1, hi2) + jnp.maximum(jnp.minimum(hi1, hi2), jnp.maximum(lo1, lo2)))
    grp_idx = jnp.zeros((1, tm), jnp.int32)
    best = group_score[0]
    for grp in range(1, N_GROUPS):
        upd = group_score[grp] > best
        grp_idx = jnp.where(upd, grp, grp_idx)
        best = jnp.where(upd, group_score[grp], best)

    def pick_group(z):
        out = rows_of(z, 0)
        for grp in range(1, N_GROUPS):
            cand = rows_of(z, grp)
            out = [jnp.where(grp_idx == grp, cand[j], out[j]) for j in range(EXPERTS_PER_GROUP)]
        return out

    sel_g = pick_group(sel)
    score_g = pick_group(scores)

    def argmax_first(vals):
        idx = jnp.zeros((1, tm), jnp.int32)
        top = vals[0]
        for j in range(1, EXPERTS_PER_GROUP):
            upd = vals[j] > top
            idx = jnp.where(upd, j, idx)
            top = jnp.where(upd, vals[j], top)
        return idx

    loc1 = argmax_first(sel_g)
    loc2 = argmax_first([jnp.where(loc1 == j, -jnp.inf, sel_g[j]) for j in range(EXPERTS_PER_GROUP)])

    def pick_local(vals, loc):
        out = vals[0]
        for j in range(1, EXPERTS_PER_GROUP):
            out = jnp.where(loc == j, vals[j], out)
        return out

    g1 = pick_local(score_g, loc1)
    g2 = pick_local(score_g, loc2)
    gsum = g1 + g2
    e1 = grp_idx * EXPERTS_PER_GROUP + loc1
    e2 = grp_idx * EXPERTS_PER_GROUP + loc2
    eid_ref[...] = jnp.concatenate([e1, e2], axis=0)
    gate_ref[...] = jnp.concatenate([g1 / gsum, g2 / gsum], axis=0)

    expert_row = lax.broadcasted_iota(jnp.int32, (N_EXPERTS, tm), 0)
    member = jnp.logical_or(expert_row == e1, expert_row == e2)
    earlier = base_ref[...] + _dot(jnp.where(member, 1.0, 0.0).astype(BF16), before_ref[...])
    r1 = jnp.sum(jnp.where(expert_row == e1, earlier, 0.0), axis=0, keepdims=True)
    r2 = jnp.sum(jnp.where(expert_row == e2, earlier, 0.0), axis=0, keepdims=True)
    rank_ref[...] = jnp.concatenate([r1, r2], axis=0).astype(jnp.int32)
    base_ref[...] = base_ref[...] + jnp.sum(jnp.where(member, 1.0, 0.0), axis=1, keepdims=True)
    cnt_ref[...] = jnp.broadcast_to(base_ref[...], cnt_ref.shape).astype(jnp.int32)


def _route(x, g, mod_l, router_w, router_bias, seq):
    n, d = x.shape
    tm = min(512, seq)
    tps = seq // tm
    lane_tile = lambda i: (0, i)
    return pl.pallas_call(
        functools.partial(_route_kernel, tm=tm),
        grid=(n // tm,),
        in_specs=[pl.BlockSpec((tm, d), lambda i: (i, 0)),
                  pl.BlockSpec((1, d), lambda i: (0, 0)),
                  _mod_spec(3, tps), _mod_spec(4, tps),
                  pl.BlockSpec((N_EXPERTS, d), lambda i: (0, 0)),
                  pl.BlockSpec((N_EXPERTS, 1), lambda i: (0, 0))],
        out_specs=[pl.BlockSpec((tm, d), lambda i: (i, 0)),
                   pl.BlockSpec((2, tm), lane_tile), pl.BlockSpec((2, tm), lane_tile),
                   pl.BlockSpec((2, tm), lane_tile),
                   pl.BlockSpec((N_EXPERTS, LANES), lambda i: (0, 0))],
        out_shape=[jax.ShapeDtypeStruct((n, d), F32),
                   jax.ShapeDtypeStruct((2, n), jnp.int32), jax.ShapeDtypeStruct((2, n), F32),
                   jax.ShapeDtypeStruct((2, n), jnp.int32),
                   jax.ShapeDtypeStruct((N_EXPERTS, LANES), jnp.int32)],
        scratch_shapes=[pltpu.VMEM((N_EXPERTS, 1), F32), pltpu.VMEM((tm, tm), BF16)],
        compiler_params=_cparams("arbitrary"),
        name="moe_router",
    )(x, g.reshape(1, d), mod_l, mod_l, router_w.T, router_bias.reshape(N_EXPERTS, 1))


def _dispatch_kernel(eid_ref, rank_ref, start_ref, h_ref, xs_in_hbm, xs_hbm, sem, *, tm):
    del xs_in_hbm

    def row_copy(t, kk):
        slot = start_ref[eid_ref[kk, t]] + rank_ref[kk, t]
        return pltpu.make_async_copy(h_ref.at[pl.ds(t, 1)], xs_hbm.at[pl.ds(slot, 1)], sem)

    def issue(t, c):
        row_copy(t, 0).start()
        row_copy(t, 1).start()
        return c

    def drain(t, c):
        row_copy(t, 0).wait()
        row_copy(t, 1).wait()
        return c

    lax.fori_loop(0, tm, issue, 0)
    lax.fori_loop(0, tm, drain, 0)


def _dispatch(h, eid, rank, pad_starts, n_slots):
    n, d = h.shape
    tm = min(1024, n)
    smem_tile = pl.BlockSpec((2, tm), lambda i: (0, i), memory_space=pltpu.SMEM)
    return pl.pallas_call(
        functools.partial(_dispatch_kernel, tm=tm),
        grid=(n // tm,),
        in_specs=[smem_tile, smem_tile,
                  pl.BlockSpec(memory_space=pltpu.SMEM),
                  pl.BlockSpec((tm, d), lambda i: (i, 0)), pl.BlockSpec(memory_space=pl.ANY)],
        out_specs=pl.BlockSpec(memory_space=pl.ANY),
        out_shape=jax.ShapeDtypeStruct((n_slots, d), F32),
        scratch_shapes=[pltpu.SemaphoreType.DMA(())],
        input_output_aliases={4: 0},
        compiler_params=_cparams("arbitrary"),
        name="moe_dispatch",
    )(eid, rank, pad_starts, h, jnp.zeros((n_slots, d), F32))


def _expert_kernel(be_ref, nused_ref, xs_ref, wg_ref, wu_ref, wd_ref, y_ref):
    del be_ref

    @pl.when(pl.program_id(0) < nused_ref[0])
    def _():
        x = xs_ref[...].astype(BF16)
        gt = _dot(x, wg_ref[...])
        up = _dot(x, wu_ref[...])
        mid = (gt * jax.nn.sigmoid(gt) * up).astype(BF16)
        y_ref[...] = _dot(mid, wd_ref[...])

    @pl.when(pl.program_id(0) >= nused_ref[0])
    def _():
        y_ref[...] = jnp.zeros_like(y_ref)


def _experts(xs, block_expert, n_used, w_gate, w_up, w_down):
    n_slots, d = xs.shape
    de = w_gate.shape[-1]
    nb = n_slots // MOE_ROWS
    grid_spec = pltpu.PrefetchScalarGridSpec(
        num_scalar_prefetch=2,
        grid=(nb,),
        in_specs=[pl.BlockSpec((MOE_ROWS, d), lambda b, be, nu: (b, 0)),
                  pl.BlockSpec((None, d, de), lambda b, be, nu: (be[b], 0, 0)),
                  pl.BlockSpec((None, d, de), lambda b, be, nu: (be[b], 0, 0)),
                  pl.BlockSpec((None, de, d), lambda b, be, nu: (be[b], 0, 0))],
        out_specs=pl.BlockSpec((MOE_ROWS, d), lambda b, be, nu: (b, 0)),
    )
    return pl.pallas_call(
        _expert_kernel,
        grid_spec=grid_spec,
        out_shape=jax.ShapeDtypeStruct((n_slots, d), F32),
        compiler_params=_cparams("arbitrary"),
        name="moe_experts",
    )(block_expert, n_used, xs, w_gate, w_up, w_down)


def _combine_kernel(eid_ref, rank_ref, start_ref, gate_ref, x_ref, g2_ref, fin_ref, yb_hbm, o_ref, buf_ref, sem,
                    *, tm, final_norm):
    def row_copy(t, kk):
        slot = start_ref[eid_ref[kk, t]] + rank_ref[kk, t]
        return pltpu.make_async_copy(yb_hbm.at[pl.ds(slot, 1)], buf_ref.at[kk, pl.ds(t, 1)], sem)

    def issue(t, c):
        row_copy(t, 0).start()
        row_copy(t, 1).start()
        return c

    def drain(t, c):
        row_copy(t, 0).wait()
        row_copy(t, 1).wait()
        return c

    lax.fori_loop(0, tm, issue, 0)
    eye = lax.broadcasted_iota(jnp.int32, (tm, tm), 0) == lax.broadcasted_iota(jnp.int32, (tm, tm), 1)
    gates = gate_ref[...]
    w0 = jnp.sum(jnp.where(eye, gates[0:1, :], 0.0), axis=1, keepdims=True)
    w1 = jnp.sum(jnp.where(eye, gates[1:2, :], 0.0), axis=1, keepdims=True)
    lax.fori_loop(0, tm, drain, 0)
    y = w0 * buf_ref[0] + w1 * buf_ref[1]
    out = x_ref[...] + g2_ref[...] * y
    if final_norm:
        out = out * lax.rsqrt(jnp.mean(out * out, axis=-1, keepdims=True) + NORM_EPS) * fin_ref[...]
    o_ref[...] = out


def _combine(yb, eid, rank, gate, pad_starts, x, mod_l, final_g, final_norm, seq):
    n, d = x.shape
    tm = min(256, seq)
    tps = seq // tm
    smem_tile = pl.BlockSpec((2, tm), lambda i: (0, i), memory_space=pltpu.SMEM)
    return pl.pallas_call(
        functools.partial(_combine_kernel, tm=tm, final_norm=final_norm),
        grid=(n // tm,),
        in_specs=[smem_tile, smem_tile,
                  pl.BlockSpec(memory_space=pltpu.SMEM),
                  pl.BlockSpec((2, tm), lambda i: (0, i)),
                  pl.BlockSpec((tm, d), lambda i: (i, 0)),
                  _mod_spec(5, tps),
                  pl.BlockSpec((1, d), lambda i: (0, 0)),
                  pl.BlockSpec(memory_space=pl.ANY)],
        out_specs=pl.BlockSpec((tm, d), lambda i: (i, 0)),
        out_shape=jax.ShapeDtypeStruct((n, d), F32),
        scratch_shapes=[pltpu.VMEM((2, tm, d), F32), pltpu.SemaphoreType.DMA(())],
        compiler_params=_cparams("arbitrary"),
        name="moe_combine",
    )(eid, rank, pad_starts, gate, x, mod_l, final_g.reshape(1, d), yb)


def _grouped_moe_residual(x, g, mod_l, router_w, router_bias, w_gate, w_up, w_down, final_g, final_norm, seq):
    n, d = x.shape
    h, eid, gate, rank, counts = _route(x, g, mod_l, router_w, router_bias, seq)
    counts = counts[:, 0]
    padded = (counts + MOE_ROWS - 1) // MOE_ROWS * MOE_ROWS
    pad_ends = jnp.cumsum(padded)
    pad_starts = (pad_ends - padded).astype(jnp.int32)
    n_slots = 2 * n + N_EXPERTS * MOE_ROWS
    nb = n_slots // MOE_ROWS
    block_start = jnp.arange(nb, dtype=jnp.int32) * MOE_ROWS
    block_expert = jnp.minimum(jnp.searchsorted(pad_ends, block_start, side='right'), N_EXPERTS - 1).astype(jnp.int32)
    n_used = (pad_ends[-1:] // MOE_ROWS).astype(jnp.int32)
    xs = _dispatch(h, eid, rank, pad_starts, n_slots)
    yb = _experts(xs, block_expert, n_used, w_gate, w_up, w_down)
    return _combine(yb, eid, rank, gate, pad_starts, x, mod_l, final_g, final_norm, seq)


def kernel(x, c, w_mod, b_mod, norm1_g, norm2_g, final_g, attn_w_in, attn_w_out, diff_lambda, diff_subln_g, rwkv_mix, rwkv_w_rkv, rwkv_w_out, rwkv_w0, rwkv_w_lora_a, rwkv_w_lora_b, rwkv_a0, rwkv_a_lora_a, rwkv_a_lora_b, rwkv_g_lora_a, rwkv_g_lora_b, rwkv_v0, rwkv_v_lora_a, rwkv_v_lora_b, rwkv_k_k, rwkv_k_a, rwkv_r_k, rwkv_lnx_g, rwkv_lnx_b, router_w, router_bias, moe_w_gate, moe_w_up, moe_w_down):
    bsz, seq, d = x.shape
    depth = w_mod.shape[0]
    n = bsz * seq
    mod = _modulation(c.astype(F32), w_mod, b_mod)
    xf = x.astype(F32).reshape(n, d)
    v_first = None
    for l in range(depth):
        mod_l = mod[l]
        if l % 2 == 0:
            e = l // 2
            lambda_init = 0.8 - 0.6 * math.exp(-0.3 * l)
            proj = _norm_matmul(xf, norm1_g[l], mod_l, attn_w_in[e].astype(BF16), seq)
            o_a = _sb_attention(proj, bsz, seq)
            o_b = _diff_attention(proj, diff_lambda[e], diff_subln_g[e], lambda_init, bsz, seq)
            xf = _out_proj([o_a, o_b], attn_w_out[e].astype(BF16), xf, mod_l, 2, seq)
        else:
            o = l // 2
            v_res = None if o == 0 else (rwkv_v0[o - 1], rwkv_v_lora_a[o - 1], rwkv_v_lora_b[o - 1])
            outs = _rwkv_proj(xf, norm1_g[l], mod_l, rwkv_mix[o], rwkv_w_rkv[o], rwkv_w0[o], rwkv_w_lora_a[o],
                              rwkv_w_lora_b[o], rwkv_a0[o], rwkv_a_lora_a[o], rwkv_a_lora_b[o], rwkv_g_lora_a[o],
                              rwkv_g_lora_b[o], v_res, seq)
            rkv, logw, a, g = outs[:4]
            vg = outs[4] if v_res is not None else None
            if v_res is None:
                v_first = rkv
            yg = _wkv(rkv, logw, a, g, vg, v_first, rwkv_k_k[o], rwkv_k_a[o], rwkv_r_k[o].reshape(d),
                      rwkv_lnx_g[o], rwkv_lnx_b[o], bsz, seq)
            xf = _out_proj([yg], rwkv_w_out[o].astype(BF16), xf, mod_l, 2, seq)
        xf = _grouped_moe_residual(xf, norm2_g[l], mod_l, router_w, router_bias, moe_w_gate[l].astype(BF16),
                                   moe_w_up[l].astype(BF16), moe_w_down[l].astype(BF16), final_g,
                                   l == depth - 1, seq)
    return xf.reshape(bsz, seq, d)
```

```python
import functools
import math

import jax
import jax.numpy as jnp
from jax import lax
from jax.experimental import pallas as pl
from jax.experimental.pallas import tpu as pltpu

F32 = jnp.float32
BF16 = jnp.bfloat16
HIGHEST = lax.Precision.HIGHEST

D_MODEL = 2048
SB_HEADS = 8
SB_HEAD_DIM = 128
DIFF_HEADS = 8
DIFF_QK_DIM = 64
DIFF_V_DIM = 128
SB_WIDTH = SB_HEADS * SB_HEAD_DIM
DIFF_QK_WIDTH = DIFF_HEADS * 2 * DIFF_QK_DIM
DIFF_V_WIDTH = DIFF_HEADS * DIFF_V_DIM
PROJ_WIDTH = 3 * SB_WIDTH + 2 * DIFF_QK_WIDTH + DIFF_V_WIDTH
RWKV_HEAD_DIM = 64
RWKV_GN_EPS = 64e-5
N_EXPERTS = 16
N_GROUPS = 4
EXPERTS_PER_GROUP = 4
D_EXPERT = D_MODEL // 2
NORM_EPS = 1e-6

LANES = 128
WKV_CHUNK = 64
WKV_LANES = 256
WKV_HEADS = WKV_LANES // RWKV_HEAD_DIM
MOE_ROWS = 256
NEG_BIG = -1e30
SB_UNDERFLOW = -105.0
VMEM_LIMIT = 56 * 1024 * 1024


def _cparams(*sem):
    return pltpu.CompilerParams(dimension_semantics=sem, vmem_limit_bytes=VMEM_LIMIT)


def _dot(a, b):
    return jnp.dot(a, b, preferred_element_type=F32)


def _dot_nt(a, b):
    return lax.dot_general(a, b, (((1,), (1,)), ((), ())), preferred_element_type=F32)


def _dot_tn(a, b):
    return lax.dot_general(a, b, (((0,), (0,)), ((), ())), preferred_element_type=F32)


def _softplus(z):
    return jnp.maximum(z, 0.0) + jnp.log1p(jnp.exp(-jnp.abs(z)))


def _modnorm(x, g, scale, shift):
    ms = jnp.mean(x * x, axis=-1, keepdims=True)
    return x * lax.rsqrt(ms + NORM_EPS) * g * (1.0 + scale) + shift


def _mod_spec(which, rows_per_batch_tile):
    return pl.BlockSpec((None, None, 1, D_MODEL), lambda i, *_: (i // rows_per_batch_tile, which, 0, 0))


def _mod_kernel(c_ref, w_ref, b_ref, o_ref):
    c = c_ref[...]
    cond = c * jax.nn.sigmoid(c)
    o_ref[...] = jnp.dot(cond, w_ref[...], preferred_element_type=F32, precision=HIGHEST) + b_ref[...]


def _modulation(c, w_mod, b_mod):
    depth, d, n6 = w_mod.shape
    bsz = c.shape[0]
    tn = 1536
    out = pl.pallas_call(
        _mod_kernel,
        grid=(depth, n6 // tn),
        in_specs=[pl.BlockSpec((bsz, d), lambda l, j: (0, 0)),
                  pl.BlockSpec((None, d, tn), lambda l, j: (l, 0, j)),
                  pl.BlockSpec((None, 1, tn), lambda l, j: (l, 0, j))],
        out_specs=pl.BlockSpec((None, bsz, tn), lambda l, j: (l, 0, j)),
        out_shape=jax.ShapeDtypeStruct((depth, bsz, n6), F32),
        compiler_params=_cparams("arbitrary", "arbitrary"),
        name="adaln_modulation",
    )(c, w_mod, b_mod.reshape(depth, 1, n6))
    return out.reshape(depth, bsz, 6, 1, d)


def _normmm_kernel(x_ref, g_ref, sh_ref, sc_ref, w_ref, o_ref, h_ref):
    @pl.when(pl.program_id(1) == 0)
    def _():
        h_ref[...] = _modnorm(x_ref[...], g_ref[...], sc_ref[...], sh_ref[...]).astype(BF16)

    o_ref[...] = _dot(h_ref[...], w_ref[...]).astype(o_ref.dtype)


def _norm_matmul(x, g, mod_l, w, seq):
    n, d = x.shape
    nout = w.shape[1]
    tm = min(1024, seq)
    tn = 512
    tps = seq // tm
    return pl.pallas_call(
        _normmm_kernel,
        grid=(n // tm, nout // tn),
        in_specs=[pl.BlockSpec((tm, d), lambda i, j: (i, 0)),
                  pl.BlockSpec((1, d), lambda i, j: (0, 0)),
                  _mod_spec(0, tps), _mod_spec(1, tps),
                  pl.BlockSpec((d, tn), lambda i, j: (0, j))],
        out_specs=pl.BlockSpec((tm, tn), lambda i, j: (i, j)),
        out_shape=jax.ShapeDtypeStruct((n, nout), BF16),
        scratch_shapes=[pltpu.VMEM((tm, d), BF16)],
        compiler_params=_cparams("arbitrary", "arbitrary"),
        name="attn_in_proj",
    )(x, g.reshape(1, d), mod_l, mod_l, w)


def _outproj_kernel(*refs, n_lhs):
    lhs = refs[:n_lhs]
    ws = refs[n_lhs:2 * n_lhs]
    x_ref, gate_ref, o_ref = refs[2 * n_lhs:]
    acc = _dot(lhs[0][...], ws[0][...])
    for a_ref, w_ref in zip(lhs[1:], ws[1:]):
        acc = acc + _dot(a_ref[...], w_ref[...])
    o_ref[...] = x_ref[...] + gate_ref[...] * acc


def _out_proj(lhs_list, w, x, mod_l, gate_idx, seq):
    n, d = x.shape
    tm = min(1024, seq)
    tn = 512
    tps = seq // tm
    n_lhs = len(lhs_list)
    kp = w.shape[0] // n_lhs
    in_specs = [pl.BlockSpec((tm, kp), lambda i, j: (i, 0)) for _ in lhs_list]
    in_specs += [pl.BlockSpec((kp, tn), functools.partial(lambda i, j, p: (p, j), p=p)) for p in range(n_lhs)]
    in_specs += [pl.BlockSpec((tm, tn), lambda i, j: (i, j)),
                 pl.BlockSpec((None, None, 1, tn), lambda i, j: (i // tps, gate_idx, 0, j))]
    return pl.pallas_call(
        functools.partial(_outproj_kernel, n_lhs=n_lhs),
        grid=(n // tm, d // tn),
        in_specs=in_specs,
        out_specs=pl.BlockSpec((tm, tn), lambda i, j: (i, j)),
        out_shape=jax.ShapeDtypeStruct((n, d), F32),
        compiler_params=_cparams("arbitrary", "arbitrary"),
        name="out_proj_residual",
    )(*lhs_list, *([w] * n_lhs), x, mod_l)


def _sb_kernel(q_ref, k_ref, v_ref, o_ref, *, tq, tk, scale):
    qi = pl.program_id(2)
    q = q_ref[...]
    jj = lax.broadcasted_iota(jnp.int32, (tk, tk), 0)
    ss = lax.broadcasted_iota(jnp.int32, (tk, tk), 1)
    later = jnp.where(jj > ss, 1.0, 0.0).astype(BF16)
    rel0 = (lax.broadcasted_iota(jnp.int32, (tq, tk), 0) - lax.broadcasted_iota(jnp.int32, (tq, tk), 1))

    def block(kb, carry, acc, masked):
        start = pl.multiple_of(kb * tk, tk)
        k = k_ref[pl.ds(start, tk), :]
        v = v_ref[pl.ds(start, tk), :]
        z = _dot_nt(q, k) * scale
        sp = _softplus(z)
        log_keep = -sp
        if masked:
            strict = (rel0 + (qi * tq - kb * tk)) > 0
            log_keep = jnp.where(strict, log_keep, 0.0)
        hi = log_keep.astype(BF16)
        lo = (log_keep - hi.astype(F32)).astype(BF16)
        cs = _dot(jnp.concatenate([hi, lo], axis=0), later)
        log_between = cs[:tq] + cs[tq:] + carry
        w = jnp.exp(z - sp + log_between)
        if masked:
            w = jnp.where(strict, w, 0.0)
        acc = acc + _dot(w.astype(BF16), v)
        carry = carry + jnp.sum(log_keep, axis=1, keepdims=True)
        return carry, acc

    carry = jnp.zeros((tq, 1), F32)
    acc = jnp.zeros((tq, SB_HEAD_DIM), F32)
    n_full = (qi * tq) // tk
    last = ((qi + 1) * tq - 1) // tk
    for m in range(max(tq // tk, 1)):
        carry, acc = block(last - m, carry, acc, True)

    def cond(c):
        return jnp.logical_and(c[0] < n_full, jnp.max(c[1]) > SB_UNDERFLOW)

    def body(c):
        carry, acc = block(n_full - 1 - c[0], c[1], c[2], False)
        return c[0] + 1, carry, acc

    _, carry, acc = lax.while_loop(cond, body, (jnp.int32(0), carry, acc))
    o_ref[...] = acc.astype(o_ref.dtype)


def _sb_attention(proj, bsz, seq):
    n = proj.shape[0]
    tq = min(256, seq)
    tk = min(128, seq)
    nq = seq // tq
    kern = functools.partial(_sb_kernel, tq=tq, tk=tk, scale=SB_HEAD_DIM ** -0.5)
    return pl.pallas_call(
        kern,
        grid=(bsz, SB_HEADS, nq),
        in_specs=[pl.BlockSpec((tq, LANES), lambda b, h, i: (b * nq + i, h)),
                  pl.BlockSpec((seq, LANES), lambda b, h, i: (b, SB_HEADS + h)),
                  pl.BlockSpec((seq, LANES), lambda b, h, i: (b, 2 * SB_HEADS + h))],
        out_specs=pl.BlockSpec((tq, LANES), lambda b, h, i: (b * nq + i, h)),
        out_shape=jax.ShapeDtypeStruct((n, SB_WIDTH), BF16),
        compiler_params=_cparams("arbitrary", "arbitrary", "arbitrary"),
        name="stick_breaking_attention",
    )(proj, proj, proj)


def _diff_kernel(q_ref, k_ref, v_ref, slope_ref, lam_ref, g_ref, o_ref, *, tq, tk, scale, lambda_init):
    qi = pl.program_id(2)
    q = q_ref[...]
    lane = lax.broadcasted_iota(jnp.int32, (tq, LANES), 1)
    zero = jnp.zeros_like(q)
    q2 = jnp.concatenate([jnp.where(lane < DIFF_QK_DIM, q, zero), jnp.where(lane >= DIFF_QK_DIM, q, zero)], axis=0)
    row = lax.broadcasted_iota(jnp.int32, (2 * tq, tk), 0)
    rel0 = jnp.where(row >= tq, row - tq, row) - lax.broadcasted_iota(jnp.int32, (2 * tq, tk), 1)
    slope = slope_ref[...]

    def block(kb, m, l, acc, masked):
        start = pl.multiple_of(kb * tk, tk)
        k = k_ref[pl.ds(start, tk), :]
        v = v_ref[pl.ds(start, tk), :]
        rel = rel0 + (qi * tq - kb * tk)
        s = _dot_nt(q2, k) * scale - slope * rel.astype(F32)
        if masked:
            s = jnp.where(rel >= 0, s, NEG_BIG)
        m_new = jnp.maximum(m, jnp.max(s, axis=1, keepdims=True))
        alpha = jnp.exp(m - m_new)
        p = jnp.exp(s - m_new)
        l = alpha * l + jnp.sum(p, axis=1, keepdims=True)
        acc = alpha * acc + _dot(p.astype(BF16), v)
        return m_new, l, acc

    m = jnp.full((2 * tq, 1), NEG_BIG, F32)
    l = jnp.zeros((2 * tq, 1), F32)
    acc = jnp.zeros((2 * tq, DIFF_V_DIM), F32)
    n_full = (qi * tq) // tk
    last = ((qi + 1) * tq - 1) // tk
    m, l, acc = lax.fori_loop(0, n_full, lambda kb, c: block(kb, c[0], c[1], c[2], False), (m, l, acc))
    n_masked = max(tq // tk, 1)
    for j in range(n_masked):
        m, l, acc = block(last - (n_masked - 1 - j), m, l, acc, True)

    lam4 = lam_ref[...]
    lam = (jnp.exp(jnp.sum(lam4[0:1] * lam4[1:2], axis=1, keepdims=True))
           - jnp.exp(jnp.sum(lam4[2:3] * lam4[3:4], axis=1, keepdims=True)) + lambda_init)
    o = acc[:tq] / l[:tq] - lam * (acc[tq:] / l[tq:])
    o = o * lax.rsqrt(jnp.mean(o * o, axis=-1, keepdims=True) + NORM_EPS) * g_ref[...] * (1.0 - lambda_init)
    o_ref[...] = o.astype(o_ref.dtype)


def _diff_attention(proj, diff_lambda, subln_g, lambda_init, bsz, seq):
    n = proj.shape[0]
    tq = min(128, seq)
    tk = min(256, seq)
    nq = seq // tq
    qoff = 3 * SB_HEADS
    slopes = 2.0 ** (-8.0 * (jnp.arange(DIFF_HEADS, dtype=F32) + 1.0) / DIFF_HEADS)
    slopes = jnp.broadcast_to(slopes[:, None, None], (DIFF_HEADS, 1, tk))
    kern = functools.partial(_diff_kernel, tq=tq, tk=tk, scale=DIFF_QK_DIM ** -0.5, lambda_init=lambda_init)
    return pl.pallas_call(
        kern,
        grid=(bsz, DIFF_HEADS, nq),
        in_specs=[pl.BlockSpec((tq, LANES), lambda b, h, i: (b * nq + i, qoff + h)),
                  pl.BlockSpec((seq, LANES), lambda b, h, i: (b, qoff + DIFF_HEADS + h)),
                  pl.BlockSpec((seq, LANES), lambda b, h, i: (b, qoff + 2 * DIFF_HEADS + h)),
                  pl.BlockSpec((None, 1, tk), lambda b, h, i: (h, 0, 0)),
                  pl.BlockSpec((4, DIFF_QK_DIM), lambda b, h, i: (0, 0)),
                  pl.BlockSpec((1, DIFF_V_DIM), lambda b, h, i: (0, 0))],
        out_specs=pl.BlockSpec((tq, LANES), lambda b, h, i: (b * nq + i, h)),
        out_shape=jax.ShapeDtypeStruct((n, DIFF_V_WIDTH), BF16),
        compiler_params=_cparams("arbitrary", "arbitrary", "arbitrary"),
        name="differential_attention",
    )(proj, proj, proj, slopes, diff_lambda, subln_g.reshape(1, DIFF_V_DIM))


RKV_TILES = 12
LORA_TILES = 4
RWKV_TN = 512


def _rwkv_proj_kernel(*refs, tm, tiles_per_seq, has_vres):
    (x_ref, xp_ref, g_ref, sh_ref, sc_ref, mix_ref, w_ref, wla_ref, ala_ref, gla_ref,
     wlb_ref, alb_ref, glb_ref, w0_ref, a0_ref) = refs[:15]
    pos = 15
    if has_vres:
        vla_ref, vlb_ref, v0_ref = refs[pos:pos + 3]
        pos += 3
    rkv_ref, logw_ref, a_out_ref, g_out_ref = refs[pos:pos + 4]
    pos += 4
    if has_vres:
        vg_out_ref = refs[pos]
        pos += 1
    lerp_ref, l1w_ref, l1a_ref, l1g_ref = refs[pos:pos + 4]
    pos += 4
    if has_vres:
        l1v_ref = refs[pos]

    i = pl.program_id(0)
    j = pl.program_id(1)

    @pl.when(j == 0)
    def _():
        g, sc, sh = g_ref[...], sc_ref[...], sh_ref[...]
        h = _modnorm(x_ref[...], g, sc, sh)
        hp = _modnorm(xp_ref[...], g, sc, sh)[7:8, :]
        hp = jnp.where(i % tiles_per_seq == 0, jnp.zeros_like(hp), hp)
        first_row = lax.broadcasted_iota(jnp.int32, h.shape, 0) == 0
        dx = jnp.where(first_row, hp, pltpu.roll(h, 1, 0)) - h
        for m in range(6):
            lerp_ref[m] = (h + dx * mix_ref[m:m + 1, :]).astype(BF16)

    @pl.when(j < RKV_TILES)
    def _():
        sel = j // (RKV_TILES // 3)
        idx = jnp.where(sel == 0, 0, sel + 1)
        rkv_ref[...] = _dot(lerp_ref[idx], w_ref[...]).astype(rkv_ref.dtype)

    @pl.when(j == RKV_TILES)
    def _():
        l1w_ref[...] = jnp.tanh(_dot(lerp_ref[1], wla_ref[...])).astype(BF16)
        l1a_ref[...] = _dot(lerp_ref[4], ala_ref[...]).astype(BF16)
        l1g_ref[...] = jax.nn.sigmoid(_dot(lerp_ref[5], gla_ref[...])).astype(BF16)
        if has_vres:
            l1v_ref[...] = _dot(lerp_ref[3], vla_ref[...]).astype(BF16)

    @pl.when(j >= RKV_TILES)
    def _():
        pre = w0_ref[...] + _dot(l1w_ref[...], wlb_ref[...])
        logw_ref[...] = -jnp.exp(-_softplus(-pre) - 0.5)
        a_out_ref[...] = jax.nn.sigmoid(a0_ref[...] + _dot(l1a_ref[...], alb_ref[...])).astype(BF16)
        g_out_ref[...] = _dot(l1g_ref[...], glb_ref[...]).astype(BF16)
        if has_vres:
            vg_out_ref[...] = jax.nn.sigmoid(v0_ref[...] + _dot(l1v_ref[...], vlb_ref[...])).astype(BF16)


def _pad_cols(w, to):
    return jnp.pad(w, ((0, 0), (0, to - w.shape[1])))


def _pad_rows(w, to):
    return jnp.pad(w, ((0, to - w.shape[0]), (0, 0)))


def _rwkv_proj(x, g, mod_l, mix, w_rkv, w0, w_la, w_lb, a0, a_la, a_lb, g_la, g_lb, v_res, seq):
    n, d = x.shape
    tm = min(512, seq)
    tn = RWKV_TN
    tps = seq // tm
    has_vres = v_res is not None
    wcat = jnp.concatenate([w_rkv[0], w_rkv[1], w_rkv[2]], axis=1).astype(BF16)
    lw = LANES
    gl = g_la.shape[1]

    def rkv_col(i, j):
        return (0, jnp.minimum(j, RKV_TILES - 1))

    def lora_col(i, j):
        return (0, jnp.clip(j - RKV_TILES, 0, LORA_TILES - 1))

    def full(shape):
        return pl.BlockSpec(shape, lambda i, j: (0, 0))

    in_specs = [pl.BlockSpec((tm, d), lambda i, j: (i, 0)),
                pl.BlockSpec((8, d), lambda i, j: (jnp.maximum(i * (tm // 8) - 1, 0), 0)),
                full((1, d)), _mod_spec(0, tps), _mod_spec(1, tps), full((6, d)),
                pl.BlockSpec((d, tn), rkv_col),
                full((d, lw)), full((d, lw)), full((d, gl)),
                pl.BlockSpec((lw, tn), lora_col), pl.BlockSpec((lw, tn), lora_col), pl.BlockSpec((gl, tn), lora_col),
                pl.BlockSpec((1, tn), lora_col), pl.BlockSpec((1, tn), lora_col)]
    args = [x, x, g.reshape(1, d), mod_l, mod_l, mix, wcat,
            _pad_cols(w_la, lw).astype(BF16), _pad_cols(a_la, lw).astype(BF16), g_la.astype(BF16),
            _pad_rows(w_lb, lw).astype(BF16), _pad_rows(a_lb, lw).astype(BF16), g_lb.astype(BF16),
            w0.reshape(1, d), a0.reshape(1, d)]
    out_tile = lambda i, j: (i, jnp.clip(j - RKV_TILES, 0, LORA_TILES - 1))
    out_specs = [pl.BlockSpec((tm, tn), lambda i, j: (i, jnp.minimum(j, RKV_TILES - 1))),
                 pl.BlockSpec((tm, tn), out_tile), pl.BlockSpec((tm, tn), out_tile), pl.BlockSpec((tm, tn), out_tile)]
    out_shape = [jax.ShapeDtypeStruct((n, 3 * d), BF16), jax.ShapeDtypeStruct((n, d), F32),
                 jax.ShapeDtypeStruct((n, d), BF16), jax.ShapeDtypeStruct((n, d), BF16)]
    scratch = [pltpu.VMEM((6, tm, d), BF16), pltpu.VMEM((tm, lw), BF16), pltpu.VMEM((tm, lw), BF16),
               pltpu.VMEM((tm, gl), BF16)]
    if has_vres:
        v0, v_la, v_lb = v_res
        in_specs += [full((d, lw)), pl.BlockSpec((lw, tn), lora_col), pl.BlockSpec((1, tn), lora_col)]
        args += [_pad_cols(v_la, lw).astype(BF16), _pad_rows(v_lb, lw).astype(BF16), v0.reshape(1, d)]
        out_specs.append(pl.BlockSpec((tm, tn), out_tile))
        out_shape.append(jax.ShapeDtypeStruct((n, d), BF16))
        scratch.append(pltpu.VMEM((tm, lw), BF16))
    return pl.pallas_call(
        functools.partial(_rwkv_proj_kernel, tm=tm, tiles_per_seq=tps, has_vres=has_vres),
        grid=(n // tm, RKV_TILES + LORA_TILES),
        in_specs=in_specs, out_specs=out_specs, out_shape=out_shape, scratch_shapes=scratch,
        compiler_params=_cparams("arbitrary", "arbitrary"),
        name="rwkv_projections",
    )(*args)


def _split3(x):
    hi = x.astype(BF16)
    r1 = x - hi.astype(F32)
    mid = r1.astype(BF16)
    lo = (r1 - mid.astype(F32)).astype(BF16)
    return hi, mid, lo


def _unit_lower_inverse(a, row, col):
    def same_block(nb):
        return (row // nb) == (col // nb)

    eye = jnp.where(row == col, 1.0, 0.0)
    b16 = same_block(16)
    ad = jnp.where(b16, a, 0.0)
    adb = ad.astype(BF16)
    a2 = _dot(adb, adb)
    a2b = a2.astype(BF16)
    a4 = _dot(a2b, a2b)
    a4b = a4.astype(BF16)
    a8 = _dot(a4b, a4b)
    t = eye + ad + a2 + _dot(adb, a2b)
    t = t + _dot(t.astype(BF16), a4b)
    t = t + _dot(t.astype(BF16), a8.astype(BF16))
    prev = b16
    for nb in (32, 64):
        cur = same_block(nb)
        off = jnp.where(jnp.logical_and(cur, jnp.logical_not(prev)), a, 0.0)
        tb = t.astype(BF16)
        t = t + _dot(_dot(tb, off.astype(BF16)).astype(BF16), tb)
        prev = cur
    return t


def _wkv_kernel(*refs, n_chunks, has_vres):
    (r_ref, k_ref, v_ref, lw_ref, a_ref, g_ref) = refs[:6]
    pos = 6
    if has_vres:
        vg_ref, vf_ref = refs[pos:pos + 2]
        pos += 2
    kk_ref, ka_ref, rk_ref, lng_ref, lnb_ref = refs[pos:pos + 5]
    pos += 5
    y_ref = refs[pos]
    s_ref = refs[pos + 1]

    L, W = WKV_CHUNK, WKV_LANES
    R = WKV_HEADS * L

    @pl.when(pl.program_id(2) == 0)
    def _():
        s_ref[...] = jnp.zeros_like(s_ref)

    lane = lax.broadcasted_iota(jnp.int32, (L, W), 1)
    head_masks = [(lane // RWKV_HEAD_DIM) == h for h in range(WKV_HEADS)]
    row = lax.broadcasted_iota(jnp.int32, (R, R), 0)
    col = lax.broadcasted_iota(jnp.int32, (R, R), 1)
    strict = row > col
    incl = row >= col
    tt = lax.broadcasted_iota(jnp.int32, (L, L), 0)
    tj = lax.broadcasted_iota(jnp.int32, (L, L), 1)
    upto = jnp.where(tj <= tt, 1.0, 0.0).astype(BF16)

    def stack(z):
        return jnp.concatenate([jnp.where(mk, z, 0.0) for mk in head_masks], axis=0)

    def head_sum(z):
        out = jnp.zeros_like(z)
        for mk in head_masks:
            out = jnp.where(mk, jnp.sum(jnp.where(mk, z, 0.0), axis=1, keepdims=True), out)
        return out

    k_k, k_a, r_k = kk_ref[...], ka_ref[...], rk_ref[...]
    ln_g, ln_b = lng_ref[...], lnb_ref[...]

    def chunk(c, carry):
        rows = pl.ds(pl.multiple_of(c * L, L), L)
        r = r_ref[rows, :].astype(F32)
        k = k_ref[rows, :].astype(F32)
        v = v_ref[rows, :].astype(F32)
        a = a_ref[rows, :].astype(F32)
        lw = lw_ref[rows, :]
        if has_vres:
            v = v + (vf_ref[rows, :].astype(F32) - v) * vg_ref[rows, :].astype(F32)
        kk = k * k_k
        kk = kk / jnp.maximum(jnp.sqrt(head_sum(kk * kk)), 1e-12)
        km = k * (1.0 + (a - 1.0) * k_a)
        bv = kk * a

        hi, mid, lo = _split3(lw)
        cs = _dot(upto, jnp.concatenate([hi, mid, lo], axis=1))
        cl = cs[:, :W] + cs[:, W:2 * W] + cs[:, 2 * W:]
        p_in = jnp.exp(cl)
        p_inv = jnp.exp(-cl)
        p_last = p_in[L - 1:L, :]
        at = -kk * jnp.exp(cl - lw)
        rt = r * p_in
        bt = bv * p_inv
        kt = km * p_inv

        s0 = s_ref[...]
        ar = jnp.concatenate([stack(at), stack(rt)], axis=0).astype(BF16)
        bk = jnp.concatenate([stack(bt), stack(kt)], axis=0).astype(BF16)
        gm = _dot_nt(ar, bk)
        a_ab = jnp.where(strict, gm[:R, :R], 0.0)
        a_ak = jnp.where(strict, gm[:R, R:], 0.0)
        a_rb = jnp.where(incl, gm[R:, :R], 0.0)
        a_rk = jnp.where(incl, gm[R:, R:], 0.0)
        ss = _dot_nt(ar, s0.astype(BF16))
        vx = stack(v)
        vxb = vx.astype(BF16)
        x = ss[:R] + _dot(a_ak.astype(BF16), vxb)
        u = _dot(_unit_lower_inverse(a_ab, row, col).astype(BF16), x.astype(BF16))
        uv = jnp.concatenate([u.astype(BF16), vxb], axis=0)
        yx = ss[R:] + _dot(jnp.concatenate([a_rb, a_rk], axis=1).astype(BF16), uv)
        y = yx[0:L]
        for h in range(1, WKV_HEADS):
            y = y + yx[h * L:(h + 1) * L]
        bkl = jnp.concatenate([stack(bt * p_last), stack(kt * p_last)], axis=0).astype(BF16)
        s_ref[...] = s0 * p_last + _dot_tn(uv, bkl)

        inv_n = 1.0 / RWKV_HEAD_DIM
        mu = head_sum(y) * inv_n
        dlt = y - mu
        var = head_sum(dlt * dlt) * inv_n
        yn = dlt * lax.rsqrt(var + RWKV_GN_EPS) * ln_g + ln_b
        yn = yn + head_sum(r * km * r_k) * v
        y_ref[rows, :] = (yn * g_ref[rows, :].astype(F32)).astype(y_ref.dtype)
        return carry

    lax.fori_loop(0, n_chunks, chunk, 0)


def _wkv(rkv, logw, a, g, vg, v_first, k_k, k_a, r_k, lnx_g, lnx_b, bsz, seq):
    n = logw.shape[0]
    d = D_MODEL
    W = WKV_LANES
    rows = min(256, seq)
    n_chunks = rows // WKV_CHUNK
    nt = seq // rows
    nw = d // W
    has_vres = vg is not None

    def tile(off):
        return pl.BlockSpec((rows, W), functools.partial(lambda b, w, t, off: (b * nt + t, off + w), off=off))

    vec = pl.BlockSpec((1, W), lambda b, w, t: (0, w))
    in_specs = [tile(0), tile(nw), tile(2 * nw), tile(0), tile(0), tile(0)]
    args = [rkv, rkv, rkv, logw, a, g]
    if has_vres:
        in_specs += [tile(0), tile(2 * nw)]
        args += [vg, v_first]
    in_specs += [vec] * 5
    args += [z.reshape(1, d) for z in (k_k, k_a, r_k, lnx_g, lnx_b)]
    return pl.pallas_call(
        functools.partial(_wkv_kernel, n_chunks=n_chunks, has_vres=has_vres),
        grid=(bsz, nw, nt),
        in_specs=in_specs,
        out_specs=tile(0),
        out_shape=jax.ShapeDtypeStruct((n, d), BF16),
        scratch_shapes=[pltpu.VMEM((W, W), F32)],
        compiler_params=_cparams("arbitrary", "arbitrary", "arbitrary"),
        name="rwkv7_chunked_state",
    )(*args)


def _route_kernel(x_ref, g_ref, sh_ref, sc_ref, rw_ref, rb_ref, h_ref, eid_ref, gate_ref, rank_ref, cnt_ref,
                  base_ref, before_ref, *, tm):
    step = pl.program_id(0)

    @pl.when(step == 0)
    def _():
        base_ref[...] = jnp.zeros_like(base_ref)
        n0 = lax.broadcasted_iota(jnp.int32, (tm, tm), 0)
        n1 = lax.broadcasted_iota(jnp.int32, (tm, tm), 1)
        before_ref[...] = jnp.where(n0 < n1, 1.0, 0.0).astype(BF16)

    h = _modnorm(x_ref[...], g_ref[...], sc_ref[...], sh_ref[...])
    h_ref[...] = h
    logits = lax.dot_general(rw_ref[...], h, (((1,), (1,)), ((), ())), preferred_element_type=F32,
                             precision=HIGHEST)
    scores = jax.nn.sigmoid(logits)
    sel = scores + rb_ref[...]

    def rows_of(z, grp):
        return [z[grp * EXPERTS_PER_GROUP + j:grp * EXPERTS_PER_GROUP + j + 1, :] for j in range(EXPERTS_PER_GROUP)]

    group_score = []
    for grp in range(N_GROUPS):
        a, b, c, d = rows_of(sel, grp)
        hi1, lo1 = jnp.maximum(a, b), jnp.minimum(a, b)
        hi2, lo2 = jnp.maximum(c, d), jnp.minimum(c, d)
        group_score.append(jnp.maximum(hi1, hi2) + jnp.maximum(jnp.minimum(hi1, hi2), jnp.maximum(lo1, lo2)))
    grp_idx = jnp.zeros((1, tm), jnp.int32)
    best = group_score[0]
    for grp in range(1, N_GROUPS):
        upd = group_score[grp] > best
        grp_idx = jnp.where(upd, grp, grp_idx)
        best = jnp.where(upd, group_score[grp], best)

    def pick_group(z):
        out = rows_of(z, 0)
        for grp in range(1, N_GROUPS):
            cand = rows_of(z, grp)
            out = [jnp.where(grp_idx == grp, cand[j], out[j]) for j in range(EXPERTS_PER_GROUP)]
        return out

    sel_g = pick_group(sel)
    score_g = pick_group(scores)

    def argmax_first(vals):
        idx = jnp.zeros((1, tm), jnp.int32)
        top = vals[0]
        for j in range(1, EXPERTS_PER_GROUP):
            upd = vals[j] > top
            idx = jnp.where(upd, j, idx)
            top = jnp.where(upd, vals[j], top)
        return idx

    loc1 = argmax_first(sel_g)
    loc2 = argmax_first([jnp.where(loc1 == j, -jnp.inf, sel_g[j]) for j in range(EXPERTS_PER_GROUP)])

    def pick_local(vals, loc):
        out = vals[0]
        for j in range(1, EXPERTS_PER_GROUP):
            out = jnp.where(loc == j, vals[j], out)
        return out

    g1 = pick_local(score_g, loc1)
    g2 = pick_local(score_g, loc2)
    gsum = g1 + g2
    e1 = grp_idx * EXPERTS_PER_GROUP + loc1
    e2 = grp_idx * EXPERTS_PER_GROUP + loc2
    eid_ref[...] = jnp.concatenate([e1, e2], axis=0)
    gate_ref[...] = jnp.concatenate([g1 / gsum, g2 / gsum], axis=0)

    expert_row = lax.broadcasted_iota(jnp.int32, (N_EXPERTS, tm), 0)
    member = jnp.logical_or(expert_row == e1, expert_row == e2)
    earlier = base_ref[...] + _dot(jnp.where(member, 1.0, 0.0).astype(BF16), before_ref[...])
    r1 = jnp.sum(jnp.where(expert_row == e1, earlier, 0.0), axis=0, keepdims=True)
    r2 = jnp.sum(jnp.where(expert_row == e2, earlier, 0.0), axis=0, keepdims=True)
    rank_ref[...] = jnp.concatenate([r1, r2], axis=0).astype(jnp.int32)
    base_ref[...] = base_ref[...] + jnp.sum(jnp.where(member, 1.0, 0.0), axis=1, keepdims=True)
    cnt_ref[...] = jnp.broadcast_to(base_ref[...], cnt_ref.shape).astype(jnp.int32)


def _route(x, g, mod_l, router_w, router_bias, seq):
    n, d = x.shape
    tm = min(512, seq)
    tps = seq // tm
    lane_tile = lambda i: (0, i)
    return pl.pallas_call(
        functools.partial(_route_kernel, tm=tm),
        grid=(n // tm,),
        in_specs=[pl.BlockSpec((tm, d), lambda i: (i, 0)),
                  pl.BlockSpec((1, d), lambda i: (0, 0)),
                  _mod_spec(3, tps), _mod_spec(4, tps),
                  pl.BlockSpec((N_EXPERTS, d), lambda i: (0, 0)),
                  pl.BlockSpec((N_EXPERTS, 1), lambda i: (0, 0))],
        out_specs=[pl.BlockSpec((tm, d), lambda i: (i, 0)),
                   pl.BlockSpec((2, tm), lane_tile), pl.BlockSpec((2, tm), lane_tile),
                   pl.BlockSpec((2, tm), lane_tile),
                   pl.BlockSpec((N_EXPERTS, LANES), lambda i: (0, 0))],
        out_shape=[jax.ShapeDtypeStruct((n, d), F32),
                   jax.ShapeDtypeStruct((2, n), jnp.int32), jax.ShapeDtypeStruct((2, n), F32),
                   jax.ShapeDtypeStruct((2, n), jnp.int32),
                   jax.ShapeDtypeStruct((N_EXPERTS, LANES), jnp.int32)],
        scratch_shapes=[pltpu.VMEM((N_EXPERTS, 1), F32), pltpu.VMEM((tm, tm), BF16)],
        compiler_params=_cparams("arbitrary"),
        name="moe_router",
    )(x, g.reshape(1, d), mod_l, mod_l, router_w.T, router_bias.reshape(N_EXPERTS, 1))


def _dispatch_kernel(eid_ref, rank_ref, start_ref, h_ref, xs_in_hbm, xs_hbm, sem, *, tm):
    del xs_in_hbm

    def row_copy(t, kk):
        slot = start_ref[eid_ref[kk, t]] + rank_ref[kk, t]
        return pltpu.make_async_copy(h_ref.at[pl.ds(t, 1)], xs_hbm.at[pl.ds(slot, 1)], sem)

    def issue(t, c):
        row_copy(t, 0).start()
        row_copy(t, 1).start()
        return c

    def drain(t, c):
        row_copy(t, 0).wait()
        row_copy(t, 1).wait()
        return c

    lax.fori_loop(0, tm, issue, 0)
    lax.fori_loop(0, tm, drain, 0)


def _dispatch(h, eid, rank, pad_starts, n_slots):
    n, d = h.shape
    tm = min(1024, n)
    smem_tile = pl.BlockSpec((2, tm), lambda i: (0, i), memory_space=pltpu.SMEM)
    return pl.pallas_call(
        functools.partial(_dispatch_kernel, tm=tm),
        grid=(n // tm,),
        in_specs=[smem_tile, smem_tile,
                  pl.BlockSpec(memory_space=pltpu.SMEM),
                  pl.BlockSpec((tm, d), lambda i: (i, 0)), pl.BlockSpec(memory_space=pl.ANY)],
        out_specs=pl.BlockSpec(memory_space=pl.ANY),
        out_shape=jax.ShapeDtypeStruct((n_slots, d), F32),
        scratch_shapes=[pltpu.SemaphoreType.DMA(())],
        input_output_aliases={4: 0},
        compiler_params=_cparams("arbitrary"),
        name="moe_dispatch",
    )(eid, rank, pad_starts, h, jnp.zeros((n_slots, d), F32))


def _expert_kernel(be_ref, nused_ref, xs_ref, wg_ref, wu_ref, wd_ref, y_ref):
    del be_ref

    @pl.when(pl.program_id(0) < nused_ref[0])
    def _():
        x = xs_ref[...].astype(BF16)
        gt = _dot(x, wg_ref[...])
        up = _dot(x, wu_ref[...])
        mid = (gt * jax.nn.sigmoid(gt) * up).astype(BF16)
        y_ref[...] = _dot(mid, wd_ref[...])

    @pl.when(pl.program_id(0) >= nused_ref[0])
    def _():
        y_ref[...] = jnp.zeros_like(y_ref)


def _experts(xs, block_expert, n_used, w_gate, w_up, w_down):
    n_slots, d = xs.shape
    de = w_gate.shape[-1]
    nb = n_slots // MOE_ROWS
    grid_spec = pltpu.PrefetchScalarGridSpec(
        num_scalar_prefetch=2,
        grid=(nb,),
        in_specs=[pl.BlockSpec((MOE_ROWS, d), lambda b, be, nu: (b, 0)),
                  pl.BlockSpec((None, d, de), lambda b, be, nu: (be[b], 0, 0)),
                  pl.BlockSpec((None, d, de), lambda b, be, nu: (be[b], 0, 0)),
                  pl.BlockSpec((None, de, d), lambda b, be, nu: (be[b], 0, 0))],
        out_specs=pl.BlockSpec((MOE_ROWS, d), lambda b, be, nu: (b, 0)),
    )
    return pl.pallas_call(
        _expert_kernel,
        grid_spec=grid_spec,
        out_shape=jax.ShapeDtypeStruct((n_slots, d), F32),
        compiler_params=_cparams("arbitrary"),
        name="moe_experts",
    )(block_expert, n_used, xs, w_gate, w_up, w_down)


def _combine_kernel(eid_ref, rank_ref, start_ref, gate_ref, x_ref, g2_ref, fin_ref, yb_hbm, o_ref, buf_ref, sem,
                    *, tm, final_norm):
    def row_copy(t, kk):
        slot = start_ref[eid_ref[kk, t]] + rank_ref[kk, t]
        return pltpu.make_async_copy(yb_hbm.at[pl.ds(slot, 1)], buf_ref.at[kk, pl.ds(t, 1)], sem)

    def issue(t, c):
        row_copy(t, 0).start()
        row_copy(t, 1).start()
        return c

    def drain(t, c):
        row_copy(t, 0).wait()
        row_copy(t, 1).wait()
        return c

    lax.fori_loop(0, tm, issue, 0)
    eye = lax.broadcasted_iota(jnp.int32, (tm, tm), 0) == lax.broadcasted_iota(jnp.int32, (tm, tm), 1)
    gates = gate_ref[...]
    w0 = jnp.sum(jnp.where(eye, gates[0:1, :], 0.0), axis=1, keepdims=True)
    w1 = jnp.sum(jnp.where(eye, gates[1:2, :], 0.0), axis=1, keepdims=True)
    lax.fori_loop(0, tm, drain, 0)
    y = w0 * buf_ref[0] + w1 * buf_ref[1]
    out = x_ref[...] + g2_ref[...] * y
    if final_norm:
        out = out * lax.rsqrt(jnp.mean(out * out, axis=-1, keepdims=True) + NORM_EPS) * fin_ref[...]
    o_ref[...] = out


def _combine(yb, eid, rank, gate, pad_starts, x, mod_l, final_g, final_norm, seq):
    n, d = x.shape
    tm = min(256, seq)
    tps = seq // tm
    smem_tile = pl.BlockSpec((2, tm), lambda i: (0, i), memory_space=pltpu.SMEM)
    return pl.pallas_call(
        functools.partial(_combine_kernel, tm=tm, final_norm=final_norm),
        grid=(n // tm,),
        in_specs=[smem_tile, smem_tile,
                  pl.BlockSpec(memory_space=pltpu.SMEM),
                  pl.BlockSpec((2, tm), lambda i: (0, i)),
                  pl.BlockSpec((tm, d), lambda i: (i, 0)),
                  _mod_spec(5, tps),
                  pl.BlockSpec((1, d), lambda i: (0, 0)),
                  pl.BlockSpec(memory_space=pl.ANY)],
        out_specs=pl.BlockSpec((tm, d), lambda i: (i, 0)),
        out_shape=jax.ShapeDtypeStruct((n, d), F32),
        scratch_shapes=[pltpu.VMEM((2, tm, d), F32), pltpu.SemaphoreType.DMA(())],
        compiler_params=_cparams("arbitrary"),
        name="moe_combine",
    )(eid, rank, pad_starts, gate, x, mod_l, final_g.reshape(1, d), yb)


def _grouped_moe_residual(x, g, mod_l, router_w, router_bias, w_gate, w_up, w_down, final_g, final_norm, seq):
    n, d = x.shape
    h, eid, gate, rank, counts = _route(x, g, mod_l, router_w, router_bias, seq)
    counts = counts[:, 0]
    padded = (counts + MOE_ROWS - 1) // MOE_ROWS * MOE_ROWS
    pad_ends = jnp.cumsum(padded)
    pad_starts = (pad_ends - padded).astype(jnp.int32)
    n_slots = 2 * n + N_EXPERTS * MOE_ROWS
    nb = n_slots // MOE_ROWS
    block_start = jnp.arange(nb, dtype=jnp.int32) * MOE_ROWS
    block_expert = jnp.minimum(jnp.searchsorted(pad_ends, block_start, side='right'), N_EXPERTS - 1).astype(jnp.int32)
    n_used = (pad_ends[-1:] // MOE_ROWS).astype(jnp.int32)
    xs = _dispatch(h, eid, rank, pad_starts, n_slots)
    yb = _experts(xs, block_expert, n_used, w_gate, w_up, w_down)
    return _combine(yb, eid, rank, gate, pad_starts, x, mod_l, final_g, final_norm, seq)


def kernel(x, c, w_mod, b_mod, norm1_g, norm2_g, final_g, attn_w_in, attn_w_out, diff_lambda, diff_subln_g, rwkv_mix, rwkv_w_rkv, rwkv_w_out, rwkv_w0, rwkv_w_lora_a, rwkv_w_lora_b, rwkv_a0, rwkv_a_lora_a, rwkv_a_lora_b, rwkv_g_lora_a, rwkv_g_lora_b, rwkv_v0, rwkv_v_lora_a, rwkv_v_lora_b, rwkv_k_k, rwkv_k_a, rwkv_r_k, rwkv_lnx_g, rwkv_lnx_b, router_w, router_bias, moe_w_gate, moe_w_up, moe_w_down):
    bsz, seq, d = x.shape
    depth = w_mod.shape[0]
    n = bsz * seq
    mod = _modulation(c.astype(F32), w_mod, b_mod)
    xf = x.astype(F32).reshape(n, d)
    v_first = None
    for l in range(depth):
        mod_l = mod[l]
        if l % 2 == 0:
            e = l // 2
            lambda_init = 0.8 - 0.6 * math.exp(-0.3 * l)
            proj = _norm_matmul(xf, norm1_g[l], mod_l, attn_w_in[e].astype(BF16), seq)
            o_a = _sb_attention(proj, bsz, seq)
            o_b = _diff_attention(proj, diff_lambda[e], diff_subln_g[e], lambda_init, bsz, seq)
            xf = _out_proj([o_a, o_b], attn_w_out[e].astype(BF16), xf, mod_l, 2, seq)
        else:
            o = l // 2
            v_res = None if o == 0 else (rwkv_v0[o - 1], rwkv_v_lora_a[o - 1], rwkv_v_lora_b[o - 1])
            outs = _rwkv_proj(xf, norm1_g[l], mod_l, rwkv_mix[o], rwkv_w_rkv[o], rwkv_w0[o], rwkv_w_lora_a[o],
                              rwkv_w_lora_b[o], rwkv_a0[o], rwkv_a_lora_a[o], rwkv_a_lora_b[o], rwkv_g_lora_a[o],
                              rwkv_g_lora_b[o], v_res, seq)
            rkv, logw, a, g = outs[:4]
            vg = outs[4] if v_res is not None else None
            if v_res is None:
                v_first = rkv
            yg = _wkv(rkv, logw, a, g, vg, v_first, rwkv_k_k[o], rwkv_k_a[o], rwkv_r_k[o].reshape(d),
                      rwkv_lnx_g[o], rwkv_lnx_b[o], bsz, seq)
            xf = _out_proj([yg], rwkv_w_out[o].astype(BF16), xf, mod_l, 2, seq)
        xf = _grouped_moe_residual(xf, norm2_g[l], mod_l, router_w, router_bias, moe_w_gate[l].astype(BF16),
                                   moe_w_up[l].astype(BF16), moe_w_down[l].astype(BF16), final_g,
                                   l == depth - 1, seq)
    return xf.reshape(bsz, seq, d)
```
